```python
import math, functools
import jax, jax.numpy as jnp
from jax import lax
import numpy as np

D_MODEL = 2048
BATCH = 1
SEQ = 16384
DEPTH = 1
DEC_BATCH = 32
DEC_SEQ = 8
PAST_LEN = 16384
PAGE_SIZE = 128

D_A = D_MODEL // 2
D_B = D_MODEL - D_A
DH = 128
H_A = D_A // DH
DK_A = DH
DV_A = DH
H_B = D_B // (2 * DH)
CONV_W = 4
CHUNK = 64
Q_BLOCK = 128
D_FF = 5632
EPS = 1e-6
D_IN = 4 * D_A + 2 * H_A + 3 * D_B
SPLIT_POINTS = [D_A, 2 * D_A, 3 * D_A, 4 * D_A, 4 * D_A + H_A, 4 * D_A + 2 * H_A,
                4 * D_A + 2 * H_A + D_B, 4 * D_A + 2 * H_A + 2 * D_B]

kernel_name = 'hymba_gdn_diffattn_macaron_step'


def _rms_norm(x, g):
    xf = x.astype(jnp.float32)
    y = xf * lax.rsqrt(jnp.mean(xf * xf, axis=-1, keepdims=True) + EPS)
    return (y * g.astype(jnp.float32)).astype(x.dtype)


def _l2_norm(x):
    return x * lax.rsqrt(jnp.sum(x * x, axis=-1, keepdims=True) + EPS)


def _swiglu(x, norm_g, w_gate, w_up, w_down):
    h = _rms_norm(x, norm_g)
    return (jax.nn.silu(h @ w_gate) * (h @ w_up)) @ w_down


def _causal_conv(xc, buf, w):
    T = xc.shape[1]
    xp = jnp.concatenate([buf.astype(xc.dtype), xc], axis=1)
    y = sum(w[j] * xp[:, j:j + T] for j in range(CONV_W))
    return jax.nn.silu(y), xp[:, -(CONV_W - 1):]


def _gated_delta_chunked(q, k, v, g, beta, s0):
    B, T, H, DK = k.shape
    DV = v.shape[-1]
    C = min(CHUNK, T)
    N = -(-T // C)
    pad = N * C - T

    def to_chunks(a):
        a = jnp.pad(a, [(0, 0), (0, pad)] + [(0, 0)] * (a.ndim - 2))
        a = a.reshape((B, N, C) + a.shape[2:])
        return jnp.moveaxis(jnp.moveaxis(a, 3, 2), 1, 0)

    q, k, v, g, beta = (to_chunks(a) for a in (q, k, v, g, beta))
    gc = jnp.cumsum(g, axis=-1)
    tri_incl = jnp.tril(jnp.ones((C, C), bool))
    tri_strict = jnp.tril(jnp.ones((C, C), bool), -1)
    diff = gc[..., :, None] - gc[..., None, :]
    decay = jnp.where(tri_incl, jnp.exp(jnp.where(tri_incl, diff, 0.0)), 0.0)
    kb = k * beta[..., None]
    A = jnp.where(tri_strict, jnp.einsum('nbhid,nbhjd->nbhij', kb, k) * decay, 0.0)
    eye = jnp.eye(C, dtype=jnp.float32)
    Tm = lax.linalg.triangular_solve(eye + A, jnp.broadcast_to(eye, A.shape),
                                     left_side=True, lower=True, unit_diagonal=True)
    u = Tm @ (v * beta[..., None])
    w = Tm @ (kb * jnp.exp(gc)[..., None])
    qk = jnp.where(tri_incl, jnp.einsum('nbhid,nbhjd->nbhij', q, k) * decay, 0.0)

    def chunk_step(S, xs):
        q_c, k_c, u_c, w_c, qk_c, gc_c = xs
        v_new = u_c - w_c @ S
        o = (q_c * jnp.exp(gc_c)[..., None]) @ S + qk_c @ v_new
        g_last = gc_c[..., -1]
        k_dec = k_c * jnp.exp(g_last[..., None] - gc_c)[..., None]
        S = S * jnp.exp(g_last)[..., None, None] + jnp.einsum('bhcd,bhce->bhde', k_dec, v_new)
        return S, o

    S, o = lax.scan(chunk_step, s0, (q, k, u, w, qk, gc))
    o = jnp.transpose(o, (1, 0, 3, 2, 4)).reshape(B, N * C, H, DV)[:, :T]
    return o, S


def _diff_attn_prompt(q, k, v, lam):
    B, S = q.shape[:2]
    nb = S // Q_BLOCK
    qb = jnp.moveaxis(q.reshape(B, nb, Q_BLOCK, H_B, 2, DH), 1, 0)
    kpos = jnp.arange(S)
    scale = DH ** -0.5

    def one_block(args):
        q_i, i = args
        s = jnp.einsum('bqhjd,bkhjd->bhjqk', q_i, k) * scale
        qpos = i * Q_BLOCK + jnp.arange(Q_BLOCK)
        s = jnp.where(kpos[None, :] <= qpos[:, None], s, -jnp.inf)
        p = jax.nn.softmax(s, axis=-1)
        wgt = p[:, :, 0] - lam * p[:, :, 1]
        return jnp.einsum('bhqk,bkhe->bqhe', wgt, v)

    o = lax.map(one_block, (qb, jnp.arange(nb)))
    return jnp.moveaxis(o, 0, 1).reshape(B, S, H_B, 2 * DH)


def _diff_attn_sample(q, k, v, lam, cache_k, cache_v, page_table, layer):
    B, T = q.shape[:2]
    scale = DH ** -0.5
    s = jnp.einsum('bqhjd,bkhjd->bhjqk', q, k) * scale
    s = jnp.where(jnp.tril(jnp.ones((T, T), bool)), s, -jnp.inf)
    m = jnp.max(s, axis=-1)
    p = jnp.exp(s - m[..., None])
    l_sum = jnp.sum(p, axis=-1)
    acc = jnp.einsum('bhjqk,bkhe->bhjqe', p, v)

    def page_step(carry, ids):
        m, l_sum, acc = carry
        kp = cache_k[layer, ids].astype(jnp.float32).reshape(B, PAGE_SIZE, H_B, 2, DH)
        vp = cache_v[layer, ids].astype(jnp.float32)
        s = jnp.einsum('bqhjd,bkhjd->bhjqk', q, kp) * scale
        m_new = jnp.maximum(m, jnp.max(s, axis=-1))
        corr = jnp.exp(m - m_new)
        p = jnp.exp(s - m_new[..., None])
        acc = acc * corr[..., None] + jnp.einsum('bhjqk,bkhe->bhjqe', p, vp)
        return (m_new, l_sum * corr + jnp.sum(p, axis=-1), acc), None

    (m, l_sum, acc), _ = lax.scan(page_step, (m, l_sum, acc), page_table.T)
    o = acc / l_sum[..., None]
    o = o[:, :, 0] - lam * o[:, :, 1]
    return jnp.transpose(o, (0, 2, 1, 3))


def _layer(x, conv_buf, s_delta, attend, lam_init,
           ffn1_norm, ffn1_w_gate, ffn1_w_up, ffn1_w_down, mix_norm, w_in, conv_w,
           A_log, dt_bias, gdn_out_norm, q_norm, k_norm, lambda_q1, lambda_k1,
           lambda_q2, lambda_k2, diff_subln, w_out, ffn2_norm, ffn2_w_gate,
           ffn2_w_up, ffn2_w_down):
    f32 = jnp.float32
    B, T, _ = x.shape
    h = x + 0.5 * _swiglu(x, ffn1_norm, ffn1_w_gate, ffn1_w_up, ffn1_w_down)
    u = _rms_norm(h, mix_norm) @ w_in
    qa, ka, va, za, ba, aa, qb, kb, vb = jnp.split(u, SPLIT_POINTS, axis=-1)

    qkv, new_buf = _causal_conv(jnp.concatenate([qa, ka, va], axis=-1), conv_buf, conv_w)
    qkv = qkv.astype(f32)
    q_a = _l2_norm(qkv[..., :D_A].reshape(B, T, H_A, DK_A)) * (DK_A ** -0.5)
    k_a = _l2_norm(qkv[..., D_A:2 * D_A].reshape(B, T, H_A, DK_A))
    v_a = qkv[..., 2 * D_A:].reshape(B, T, H_A, DV_A)
    beta = jax.nn.sigmoid(ba.astype(f32))
    g = -jnp.exp(A_log.astype(f32)) * jax.nn.softplus(aa.astype(f32) + dt_bias.astype(f32))
    o_a, new_s = _gated_delta_chunked(q_a, k_a, v_a, g, beta, s_delta.astype(f32))
    o_a = _rms_norm(o_a, gdn_out_norm) * jax.nn.silu(za.astype(f32).reshape(B, T, H_A, DV_A))

    q_b = _rms_norm(qb.astype(f32).reshape(B, T, H_B, 2, DH), q_norm)
    k_b = _rms_norm(kb.astype(f32).reshape(B, T, H_B, 2, DH), k_norm)
    v_b = vb.astype(f32).reshape(B, T, H_B, 2 * DH)
    lam = (jnp.exp(jnp.sum(lambda_q1.astype(f32) * lambda_k1.astype(f32)))
           - jnp.exp(jnp.sum(lambda_q2.astype(f32) * lambda_k2.astype(f32))) + lam_init)
    o_b = attend(q_b, k_b, v_b, lam)
    o_b = _rms_norm(o_b, diff_subln) * (1.0 - lam_init)

    mix = jnp.concatenate([o_a.reshape(B, T, D_A), o_b.reshape(B, T, D_B)], axis=-1).astype(x.dtype) @ w_out
    h = h + mix
    y = h + 0.5 * _swiglu(h, ffn2_norm, ffn2_w_gate, ffn2_w_up, ffn2_w_down)
    k_rows = k_b.reshape(B, T, H_B, 2 * DH).astype(x.dtype)
    return y, k_rows, v_b.astype(x.dtype), new_s, new_buf


def setup_inputs(seed: int = 0) -> dict:
    key = jax.random.key(seed)
    ks = jax.random.split(key, 32)
    f32 = jnp.float32
    n_pages = PAST_LEN // PAGE_SIZE
    n_used = DEC_BATCH * n_pages
    n_pool = n_used + max(1, n_used // 4)

    def nrm(k, shape, scale):
        return jax.random.normal(k, shape, f32) * scale

    def gain(k, n):
        return 1.0 + 0.02 * jax.random.normal(k, (DEPTH, n), f32)

    page_table = jax.random.permutation(ks[4], n_pool)[:n_used].reshape(DEC_BATCH, n_pages).astype(jnp.int32)
    dt = jnp.exp(jax.random.uniform(ks[15], (DEPTH, H_A), f32, math.log(1e-3), math.log(1e-1)))
    dt_bias = dt + jnp.log(-jnp.expm1(-dt))
    A_log = jnp.log(jax.random.uniform(ks[14], (DEPTH, H_A), f32, 1.0, 16.0))
    return {
        'x_prompt': nrm(ks[0], (BATCH, SEQ, D_MODEL), 1.0),
        'x_sample': nrm(ks[1], (DEC_BATCH, DEC_SEQ, D_MODEL), 1.0),
        'cache_k': nrm(ks[2], (DEPTH, n_pool, PAGE_SIZE, H_B, 2 * DH), 1.0),
        'cache_v': nrm(ks[3], (DEPTH, n_pool, PAGE_SIZE, H_B, 2 * DH), 1.0),
        'page_table': page_table,
        'state_delta': nrm(ks[5], (DEPTH, DEC_BATCH, H_A, DK_A, DV_A), 0.1),
        'state_conv': nrm(ks[6], (DEPTH, DEC_BATCH, CONV_W - 1, 3 * D_A), 1.0),
        'ffn1_norm': gain(ks[7], D_MODEL),
        'ffn1_w_gate': nrm(ks[8], (DEPTH, D_MODEL, D_FF), D_MODEL ** -0.5),
        'ffn1_w_up': nrm(ks[9], (DEPTH, D_MODEL, D_FF), D_MODEL ** -0.5),
        'ffn1_w_down': nrm(ks[10], (DEPTH, D_FF, D_MODEL), D_FF ** -0.5),
        'mix_norm': gain(ks[11], D_MODEL),
        'w_in': nrm(ks[12], (DEPTH, D_MODEL, D_IN), D_MODEL ** -0.5),
        'conv_w': nrm(ks[13], (DEPTH, CONV_W, 3 * D_A), CONV_W ** -0.5),
        'A_log': A_log,
        'dt_bias': dt_bias,
        'gdn_out_norm': gain(ks[16], DV_A),
        'q_norm': gain(ks[17], DH),
        'k_norm': gain(ks[18], DH),
        'lambda_q1': nrm(ks[19], (DEPTH, DH), 0.1),
        'lambda_k1': nrm(ks[20], (DEPTH, DH), 0.1),
        'lambda_q2': nrm(ks[21], (DEPTH, DH), 0.1),
        'lambda_k2': nrm(ks[22], (DEPTH, DH), 0.1),
        'diff_subln': gain(ks[23], 2 * DH),
        'w_out': nrm(ks[24], (DEPTH, D_A + D_B, D_MODEL), (D_A + D_B) ** -0.5),
        'ffn2_norm': gain(ks[25], D_MODEL),
        'ffn2_w_gate': nrm(ks[26], (DEPTH, D_MODEL, D_FF), D_MODEL ** -0.5),
        'ffn2_w_up': nrm(ks[27], (DEPTH, D_MODEL, D_FF), D_MODEL ** -0.5),
        'ffn2_w_down': nrm(ks[28], (DEPTH, D_FF, D_MODEL), D_FF ** -0.5),
    }


def reference(x_prompt, x_sample, cache_k, cache_v, page_table, state_delta, state_conv,
              ffn1_norm, ffn1_w_gate, ffn1_w_up, ffn1_w_down, mix_norm, w_in, conv_w,
              A_log, dt_bias, gdn_out_norm, q_norm, k_norm, lambda_q1, lambda_k1,
              lambda_q2, lambda_k2, diff_subln, w_out, ffn2_norm, ffn2_w_gate,
              ffn2_w_up, ffn2_w_down):
    yp, ys = x_prompt, x_sample
    kp_l, vp_l, ks_l, vs_l, sp_l, bp_l, ss_l, bs_l = [], [], [], [], [], [], [], []
    for l in range(DEPTH):
        lp = (ffn1_norm[l], ffn1_w_gate[l], ffn1_w_up[l], ffn1_w_down[l], mix_norm[l], w_in[l],
              conv_w[l], A_log[l], dt_bias[l], gdn_out_norm[l], q_norm[l], k_norm[l],
              lambda_q1[l], lambda_k1[l], lambda_q2[l], lambda_k2[l], diff_subln[l], w_out[l],
              ffn2_norm[l], ffn2_w_gate[l], ffn2_w_up[l], ffn2_w_down[l])
        lam_init = 0.8 - 0.6 * math.exp(-0.3 * l)
        buf0 = jnp.zeros((yp.shape[0], CONV_W - 1, 3 * D_A), yp.dtype)
        s0 = jnp.zeros((yp.shape[0], H_A, DK_A, DV_A), jnp.float32)
        yp, kp, vp, sp, bp = _layer(yp, buf0, s0, _diff_attn_prompt, lam_init, *lp)
        attend_s = functools.partial(_diff_attn_sample, cache_k=cache_k, cache_v=cache_v,
                                     page_table=page_table, layer=l)
        ys, k_s, v_s, s_s, b_s = _layer(ys, state_conv[l], state_delta[l], attend_s, lam_init, *lp)
        kp_l.append(kp); vp_l.append(vp); sp_l.append(sp); bp_l.append(bp)
        ks_l.append(k_s); vs_l.append(v_s); ss_l.append(s_s); bs_l.append(b_s)
    new_k_prompt = jnp.stack(kp_l)
    new_v_prompt = jnp.stack(vp_l)
    new_k_sample = jnp.stack(ks_l)
    new_v_sample = jnp.stack(vs_l)
    new_state_delta_prompt = jnp.stack(sp_l)
    new_state_conv_prompt = jnp.stack(bp_l)
    new_state_delta_sample = jnp.stack(ss_l)
    new_state_conv_sample = jnp.stack(bs_l)
    return (yp, ys, new_k_prompt, new_v_prompt, new_k_sample, new_v_sample,
            new_state_delta_prompt, new_state_conv_prompt, new_state_delta_sample, new_state_conv_sample)
```

```python
import functools
import math

import jax
import jax.numpy as jnp
from jax import lax
from jax.experimental import pallas as pl
from jax.experimental.pallas import tpu as pltpu

F32 = jnp.float32
BF16 = jnp.bfloat16

D_MODEL = 2048
D_A = 1024
D_B = 1024
DH = 128
H_A = D_A // DH
H_B = D_B // (2 * DH)
CONV_W = 4
D_FF = 5632
EPS = 1e-6
PAGE_SIZE = 128
LANES = 128
GATE_LANE = 8
TRI_BASE = 16

MIB = 1024 * 1024


def _params(semantics, vmem_mib):
    return pltpu.CompilerParams(dimension_semantics=semantics, vmem_limit_bytes=vmem_mib * MIB)


def _rms(x, g):
    return x * lax.rsqrt(jnp.mean(x * x, axis=-1, keepdims=True) + EPS) * g


def _silu(x):
    return x * jax.nn.sigmoid(x)


def _softplus(x):
    return jnp.maximum(x, 0.0) + jnp.log1p(jnp.exp(-jnp.abs(x)))


def _ffn_body(x_ref, g_ref, wg_ref, wu_ref, wd_ref, *rest, nf, with_norm):
    if with_norm:
        ng_ref, o_ref, on_ref, xn_ref = rest
    else:
        o_ref, xn_ref = rest
    f = pl.program_id(1)

    @pl.when(f == 0)
    def _():
        xn_ref[...] = _rms(x_ref[...], g_ref[...]).astype(BF16)
        o_ref[...] = jnp.zeros_like(o_ref)

    xn = xn_ref[...]
    a = jnp.dot(xn, wg_ref[...], preferred_element_type=F32)
    b = jnp.dot(xn, wu_ref[...], preferred_element_type=F32)
    hid = (_silu(a) * b).astype(BF16)
    o_ref[...] += jnp.dot(hid, wd_ref[...], preferred_element_type=F32)

    @pl.when(f == nf - 1)
    def _():
        y = x_ref[...] + 0.5 * o_ref[...]
        o_ref[...] = y
        if with_norm:
            on_ref[...] = _rms(y, ng_ref[...]).astype(BF16)


def _ffn(x, g, wg, wu, wd, next_g=None, *, tm, tf=512):
    t, d = x.shape
    nf = D_FF // tf
    with_norm = next_g is not None
    in_specs = [
        pl.BlockSpec((tm, d), lambda i, f: (i, 0)),
        pl.BlockSpec((1, d), lambda i, f: (0, 0)),
        pl.BlockSpec((d, tf), lambda i, f: (0, f)),
        pl.BlockSpec((d, tf), lambda i, f: (0, f)),
        pl.BlockSpec((tf, d), lambda i, f: (f, 0)),
    ]
    args = [x, g.reshape(1, d), wg, wu, wd]
    out_shape = [jax.ShapeDtypeStruct((t, d), F32)]
    out_specs = [pl.BlockSpec((tm, d), lambda i, f: (i, 0))]
    if with_norm:
        in_specs.append(pl.BlockSpec((1, d), lambda i, f: (0, 0)))
        args.append(next_g.reshape(1, d))
        out_shape.append(jax.ShapeDtypeStruct((t, d), BF16))
        out_specs.append(pl.BlockSpec((tm, d), lambda i, f: (i, 0)))
    res = pl.pallas_call(
        functools.partial(_ffn_body, nf=nf, with_norm=with_norm),
        grid=(t // tm, nf),
        in_specs=in_specs,
        out_specs=out_specs,
        out_shape=out_shape,
        scratch_shapes=[pltpu.VMEM((tm, d), BF16)],
        compiler_params=_params(("parallel", "arbitrary"), 48),
        name="ffn",
    )(*args)
    return res if with_norm else res[0]


def _head_norm_store(acc, g_ref, scale, refs):
    for c in range(acc.shape[1] // DH):
        seg = acc[:, c * DH:(c + 1) * DH]
        y = _rms(seg, g_ref[...])
        if scale != 1.0:
            y = y * scale
        for r in refs:
            r[:, c * DH:(c + 1) * DH] = y.astype(r.dtype)


def _proj_body(x_ref, w_ref, *rest, mode):
    acc = jnp.dot(x_ref[...], w_ref[...], preferred_element_type=F32)
    if mode == "plain":
        (o_ref,) = rest
        o_ref[...] = acc
    elif mode == "qnorm":
        g_ref, o_ref = rest
        _head_norm_store(acc, g_ref, DH ** -0.5, (o_ref,))
    elif mode == "knorm":
        g_ref, o32_ref, o16_ref = rest
        _head_norm_store(acc, g_ref, 1.0, (o32_ref, o16_ref))
    else:
        o32_ref, o16_ref = rest
        o32_ref[...] = acc
        o16_ref[...] = acc.astype(BF16)


def _proj(xn, w, mode, gain=None, *, tm, tn):
    t, d = xn.shape
    n = w.shape[1]
    in_specs = [pl.BlockSpec((tm, d), lambda i, j: (i, 0)), pl.BlockSpec((d, tn), lambda i, j: (0, j))]
    args = [xn, w]
    if mode in ("qnorm", "knorm"):
        in_specs.append(pl.BlockSpec((1, DH), lambda i, j: (0, 0)))
        args.append(gain.reshape(1, DH))
    o_spec = pl.BlockSpec((tm, tn), lambda i, j: (i, j))
    if mode == "plain":
        out_shape, out_specs = [jax.ShapeDtypeStruct((t, n), F32)], [o_spec]
    elif mode == "qnorm":
        out_shape, out_specs = [jax.ShapeDtypeStruct((t, n), BF16)], [o_spec]
    else:
        out_shape = [jax.ShapeDtypeStruct((t, n), F32), jax.ShapeDtypeStruct((t, n), BF16)]
        out_specs = [o_spec, o_spec]
    return pl.pallas_call(
        functools.partial(_proj_body, mode=mode),
        grid=(t // tm, n // tn),
        in_specs=in_specs,
        out_specs=out_specs,
        out_shape=out_shape,
        compiler_params=_params(("parallel", "arbitrary"), 40),
        name="proj_" + mode,
    )(*args)


def _split(x):
    hi = x.astype(BF16)
    lo = (x - hi.astype(F32)).astype(BF16)
    return hi, lo


def _dot3(a, b, dims=(((1,), (0,)), ((), ()))):
    ah, al = _split(a)
    bh, bl = _split(b)
    dg = functools.partial(lax.dot_general, dimension_numbers=dims, preferred_element_type=F32)
    return dg(ah, bh) + (dg(ah, bl) + dg(al, bh))


_NT = (((1,), (1,)), ((), ()))
_TN = (((0,), (0,)), ((), ()))


def _tri_inv(a):
    c = a.shape[0]
    base = min(TRI_BASE, c)
    row = lax.broadcasted_iota(jnp.int32, (c, c), 0)
    col = lax.broadcasted_iota(jnp.int32, (c, c), 1)
    eye = (row == col).astype(F32)

    def neumann(m, order):
        x = eye - m
        p = m
        k = 1
        while 2 * k < order:
            p = _dot3(p, p)
            x = x + _dot3(x, p)
            k *= 2
        return x

    if c == base:
        return neumann(a, base)
    shift = base.bit_length() - 1
    same = (row >> shift) == (col >> shift)
    dinv = neumann(jnp.where(same, a, 0.0), base)
    e = _dot3(dinv, jnp.where(same, 0.0, a))
    return _dot3(neumann(e, c // base), dinv)


def _gdn_body(qkv_ref, z_ref, gt_ref, buf_ref, cw_ref, ad_ref, gn_ref, s0_ref,
              o_ref, s_ref, xp_ref, *, tb, chunk):
    t = pl.program_id(1)

    @pl.when(t == 0)
    def _():
        s_ref[...] = s0_ref[...]
        xp_ref[0:8, :] = buf_ref[0]

    xp_ref[8:8 + tb, :] = qkv_ref[...]

    def conv(c0):
        cols = slice(c0, c0 + DH)
        y = cw_ref[0:1, cols] * xp_ref[5:5 + tb, cols]
        y = y + cw_ref[1:2, cols] * xp_ref[6:6 + tb, cols]
        y = y + cw_ref[2:3, cols] * xp_ref[7:7 + tb, cols]
        y = y + cw_ref[3:4, cols] * xp_ref[8:8 + tb, cols]
        return _silu(y)

    def l2(x):
        return x * lax.rsqrt(jnp.sum(x * x, axis=-1, keepdims=True) + EPS)

    gt = gt_ref[...]
    beta = jax.nn.sigmoid(gt)
    g = -jnp.exp(ad_ref[0:1, :]) * _softplus(gt + ad_ref[1:2, :])
    row = lax.broadcasted_iota(jnp.int32, (tb, tb), 0)
    col = lax.broadcasted_iota(jnp.int32, (tb, tb), 1)
    cshift = chunk.bit_length() - 1
    seg = ((col <= row) & ((row >> cshift) == (col >> cshift))).astype(F32)
    gc = _dot3(seg, g)
    pad = max(LANES - tb, 0)
    gc_rows = jnp.concatenate([gc, jnp.zeros((pad, LANES), F32)], axis=0) if pad else gc
    gct = gc_rows.T

    crow = lax.broadcasted_iota(jnp.int32, (chunk, chunk), 0)
    ccol = lax.broadcasted_iota(jnp.int32, (chunk, chunk), 1)
    tri_incl = ccol <= crow
    tri_strict = ccol < crow

    for h in range(H_A):
        q = l2(conv(h * DH)) * (DH ** -0.5)
        k = l2(conv(D_A + h * DH))
        v = conv(2 * D_A + h * DH)
        bcol = beta[:, h:h + 1]
        gcol = gc[:, GATE_LANE + h:GATE_LANE + h + 1]
        kb = k * bcol
        vb = v * bcol
        egc = jnp.exp(gcol)
        s = s_ref[0, h]
        for c in range(tb // chunk):
            r = slice(c * chunk, (c + 1) * chunk)
            grow = gct[GATE_LANE + h:GATE_LANE + h + 1, r]
            diff = gcol[r] - grow
            decay = jnp.where(tri_incl, jnp.exp(jnp.where(tri_incl, diff, 0.0)), 0.0)
            amat = jnp.where(tri_strict, _dot3(kb[r], k[r], _NT) * decay, 0.0)
            tinv = _tri_inv(amat)
            u = _dot3(tinv, vb[r])
            w = _dot3(tinv, kb[r] * egc[r])
            qk = jnp.where(tri_incl, _dot3(q[r], k[r], _NT) * decay, 0.0)
            v_new = u - _dot3(w, s)
            o = _dot3(q[r] * egc[r], s) + _dot3(qk, v_new)
            g_last = gcol[(c + 1) * chunk - 1:(c + 1) * chunk]
            k_dec = k[r] * jnp.exp(g_last - gcol[r])
            s = s * jnp.exp(g_last) + _dot3(k_dec, v_new, _TN)
            gate = _silu(z_ref[r, h * DH:(h + 1) * DH])
            o_ref[r, h * DH:(h + 1) * DH] = (_rms(o, gn_ref[...]) * gate).astype(BF16)
        s_ref[0, h] = s

    xp_ref[0:8, :] = xp_ref[tb:tb + 8, :]


def _gdn(ua, gates, buf8, conv_w, a_log, dt_bias, out_norm, s0, *, batch, seq, tb, chunk):
    nt = seq // tb
    ad = jnp.zeros((8, LANES), F32)
    ad = ad.at[0, GATE_LANE:GATE_LANE + H_A].set(a_log).at[1, GATE_LANE:GATE_LANE + H_A].set(dt_bias)
    cw = jnp.zeros((8, 3 * D_A), F32).at[:CONV_W].set(conv_w)
    return pl.pallas_call(
        functools.partial(_gdn_body, tb=tb, chunk=chunk),
        grid=(batch, nt),
        in_specs=[
            pl.BlockSpec((tb, 3 * D_A), lambda b, t: (b * nt + t, 0)),
            pl.BlockSpec((tb, D_A), lambda b, t: (b * nt + t, 3)),
            pl.BlockSpec((tb, LANES), lambda b, t: (b * nt + t, 0)),
            pl.BlockSpec((1, 8, 3 * D_A), lambda b, t: (b, 0, 0)),
            pl.BlockSpec((8, 3 * D_A), lambda b, t: (0, 0)),
            pl.BlockSpec((8, LANES), lambda b, t: (0, 0)),
            pl.BlockSpec((1, DH), lambda b, t: (0, 0)),
            pl.BlockSpec((1, H_A, DH, DH), lambda b, t: (b, 0, 0, 0)),
        ],
        out_specs=[
            pl.BlockSpec((tb, D_A), lambda b, t: (b * nt + t, 0)),
            pl.BlockSpec((1, H_A, DH, DH), lambda b, t: (b, 0, 0, 0)),
        ],
        out_shape=[
            jax.ShapeDtypeStruct((batch * seq, D_A), BF16),
            jax.ShapeDtypeStruct((batch, H_A, DH, DH), F32),
        ],
        scratch_shapes=[pltpu.VMEM((tb + 8, 3 * D_A), F32)],
        compiler_params=_params(("parallel", "arbitrary"), 40),
        name="gdn",
    )(ua, ua, gates, buf8, cw, ad, out_norm.reshape(1, DH), s0)


def _lambda(lam_ref, lam_init):
    l1 = jnp.sum(lam_ref[0:1, :] * lam_ref[1:2, :], axis=-1, keepdims=True)
    l2 = jnp.sum(lam_ref[2:3, :] * lam_ref[3:4, :], axis=-1, keepdims=True)
    return jnp.exp(l1) - jnp.exp(l2) + lam_init


def _attn_prompt_body(lam_ref, sg_ref, q_ref, k_ref, v_ref, o_ref, m_ref, l_ref, acc_ref, *, tq, lam_init):
    qi = pl.program_id(1)
    m_ref[...] = jnp.full(m_ref.shape, -jnp.inf, F32)
    l_ref[...] = jnp.zeros(l_ref.shape, F32)
    acc_ref[...] = jnp.zeros(acc_ref.shape, F32)

    def block(j, masked):
        start = pl.multiple_of(j * tq, tq)
        ks = k_ref[pl.ds(start, tq), :]
        vs = v_ref[pl.ds(start, tq), :]
        for sub in range(2):
            cols = slice(sub * DH, (sub + 1) * DH)
            s = lax.dot_general(q_ref[:, cols], ks[:, cols], _NT, preferred_element_type=F32)
            if masked:
                row = lax.broadcasted_iota(jnp.int32, (tq, tq), 0)
                col = lax.broadcasted_iota(jnp.int32, (tq, tq), 1)
                s = jnp.where(col <= row, s, -jnp.inf)
            m_prev = m_ref[sub][:, 0:1]
            m_new = jnp.maximum(m_prev, jnp.max(s, axis=-1, keepdims=True))
            corr = jnp.exp(m_prev - m_new)
            p = jnp.exp(s - m_new)
            l_new = corr * l_ref[sub][:, 0:1] + jnp.sum(p, axis=-1, keepdims=True)
            acc_ref[sub] = acc_ref[sub] * corr + jnp.dot(p.astype(BF16), vs, preferred_element_type=F32)
            m_ref[sub] = jnp.broadcast_to(m_new, (tq, LANES))
            l_ref[sub] = jnp.broadcast_to(l_new, (tq, LANES))

    def loop_body(j, carry):
        block(j, False)
        return carry

    lax.fori_loop(0, qi, loop_body, 0)
    block(qi, True)

    lam = _lambda(lam_ref, lam_init)
    o = acc_ref[0] / l_ref[0][:, 0:1] - lam * (acc_ref[1] / l_ref[1][:, 0:1])
    o_ref[...] = (_rms(o, sg_ref[...]) * (1.0 - lam_init)).astype(BF16)


def _attn_prompt(q, k, v, lam_rows, subln, lam_init, *, tq):
    s_len = q.shape[0]
    hw = 2 * DH
    return pl.pallas_call(
        functools.partial(_attn_prompt_body, tq=tq, lam_init=lam_init),
        grid=(H_B, s_len // tq),
        in_specs=[
            pl.BlockSpec((8, DH), lambda h, i: (0, 0)),
            pl.BlockSpec((1, hw), lambda h, i: (0, 0)),
            pl.BlockSpec((tq, hw), lambda h, i: (i, h)),
            pl.BlockSpec((s_len, hw), lambda h, i: (0, h)),
            pl.BlockSpec((s_len, hw), lambda h, i: (0, h)),
        ],
        out_specs=pl.BlockSpec((tq, hw), lambda h, i: (i, h)),
        out_shape=jax.ShapeDtypeStruct((s_len, H_B * hw), BF16),
        scratch_shapes=[
            pltpu.VMEM((2, tq, LANES), F32),
            pltpu.VMEM((2, tq, LANES), F32),
            pltpu.VMEM((2, tq, hw), F32),
        ],
        compiler_params=_params(("parallel", "arbitrary"), 52),
        name="attn_prompt",
    )(lam_rows, subln.reshape(1, hw), q, k, v)


def _attn_sample_body(pt_ref, lam_ref, sg_ref, q_ref, kn_ref, vn_ref, *rest, group, n_steps, n_new, lam_init):
    kc_refs = rest[:group]
    vc_refs = rest[group:2 * group]
    o_ref, m_ref, l_ref, acc_ref = rest[2 * group:]
    step = pl.program_id(1)
    rows = q_ref.shape[1]
    q = q_ref[0]

    def update(s, vals):
        m_prev = m_ref[:, 0:1]
        m_new = jnp.maximum(m_prev, jnp.max(s, axis=-1, keepdims=True))
        corr = jnp.exp(m_prev - m_new)
        p = jnp.exp(s - m_new)
        l_new = corr * l_ref[:, 0:1] + jnp.sum(p, axis=-1, keepdims=True)
        pb = p.astype(BF16)
        pv = None
        width = s.shape[1] // len(vals)
        for i, val in enumerate(vals):
            part = jnp.dot(pb[:, i * width:(i + 1) * width], val, preferred_element_type=F32)
            pv = part if pv is None else pv + part
        acc_ref[...] = acc_ref[...] * corr + pv
        m_ref[...] = jnp.broadcast_to(m_new, m_ref.shape)
        l_ref[...] = jnp.broadcast_to(l_new, l_ref.shape)

    @pl.when(step == 0)
    def _():
        m_ref[...] = jnp.full(m_ref.shape, -jnp.inf, F32)
        l_ref[...] = jnp.zeros(l_ref.shape, F32)
        acc_ref[...] = jnp.zeros(acc_ref.shape, F32)
        kn = kn_ref[0]
        s = lax.dot_general(q, kn, _NT, preferred_element_type=F32)
        tok = lax.broadcasted_iota(jnp.int32, s.shape, 0) & (n_new - 1)
        key = lax.broadcasted_iota(jnp.int32, s.shape, 1)
        s = jnp.where(key <= tok, s, -jnp.inf)
        update(s, [vn_ref[0]])

    s_parts = [lax.dot_general(q, kc[0].astype(BF16), _NT, preferred_element_type=F32) for kc in kc_refs]
    update(jnp.concatenate(s_parts, axis=-1), [vc[0].astype(BF16) for vc in vc_refs])

    @pl.when(step == n_steps - 1)
    def _():
        lam = _lambda(lam_ref, lam_init)
        hw = 2 * DH
        for h in range(H_B):
            r0 = h * 2 * n_new
            a0 = acc_ref[r0:r0 + n_new, h * hw:(h + 1) * hw] / l_ref[r0:r0 + n_new, 0:1]
            a1 = acc_ref[r0 + n_new:r0 + 2 * n_new, h * hw:(h + 1) * hw] / l_ref[r0 + n_new:r0 + 2 * n_new, 0:1]
            o = a0 - lam * a1
            o_ref[0, :, h * hw:(h + 1) * hw] = (_rms(o, sg_ref[...]) * (1.0 - lam_init)).astype(BF16)


def _attn_sample(qbd, k_new, v_new, cache_k, cache_v, page_ids, lam_rows, subln, lam_init, *, n_pages, group):
    batch, rows, width = qbd.shape
    n_new = rows // (2 * H_B)
    n_steps = n_pages // group
    hw = 2 * DH

    def cache_spec(g):
        return pl.BlockSpec((1, PAGE_SIZE, width), lambda b, s, pt: (pt[b * n_pages + s * group + g], 0, 0))

    in_specs = [
        pl.BlockSpec((8, DH), lambda b, s, pt: (0, 0)),
        pl.BlockSpec((1, hw), lambda b, s, pt: (0, 0)),
        pl.BlockSpec((1, rows, width), lambda b, s, pt: (b, 0, 0)),
        pl.BlockSpec((1, k_new.shape[1], width), lambda b, s, pt: (b, 0, 0)),
        pl.BlockSpec((1, v_new.shape[1], width), lambda b, s, pt: (b, 0, 0)),
    ] + [cache_spec(g) for g in range(group)] + [cache_spec(g) for g in range(group)]
    grid_spec = pltpu.PrefetchScalarGridSpec(
        num_scalar_prefetch=1,
        grid=(batch, n_steps),
        in_specs=in_specs,
        out_specs=pl.BlockSpec((1, n_new, width), lambda b, s, pt: (b, 0, 0)),
        scratch_shapes=[
            pltpu.VMEM((rows, LANES), F32),
            pltpu.VMEM((rows, LANES), F32),
            pltpu.VMEM((rows, width), F32),
        ],
    )
    return pl.pallas_call(
        functools.partial(_attn_sample_body, group=group, n_steps=n_steps, n_new=n_new, lam_init=lam_init),
        grid_spec=grid_spec,
        out_shape=jax.ShapeDtypeStruct((batch, n_new, width), BF16),
        compiler_params=_params(("parallel", "arbitrary"), 40),
        name="attn_sample",
    )(page_ids, lam_rows, subln.reshape(1, hw), qbd, k_new, v_new,
      *([cache_k] * group), *([cache_v] * group))


def _outproj_body(h_ref, a_ref, b_ref, wa_ref, wb_ref, o_ref):
    mix = jnp.dot(a_ref[...], wa_ref[...], preferred_element_type=F32)
    mix = mix + jnp.dot(b_ref[...], wb_ref[...], preferred_element_type=F32)
    o_ref[...] = h_ref[...] + mix


def _outproj(h, oa, ob, wa, wb, *, tm, tn=512):
    t, d = h.shape
    return pl.pallas_call(
        _outproj_body,
        grid=(t // tm, d // tn),
        in_specs=[
            pl.BlockSpec((tm, tn), lambda i, j: (i, j)),
            pl.BlockSpec((tm, D_A), lambda i, j: (i, 0)),
            pl.BlockSpec((tm, D_B), lambda i, j: (i, 0)),
            pl.BlockSpec((D_A, tn), lambda i, j: (0, j)),
            pl.BlockSpec((D_B, tn), lambda i, j: (0, j)),
        ],
        out_specs=pl.BlockSpec((tm, tn), lambda i, j: (i, j)),
        out_shape=jax.ShapeDtypeStruct((t, d), F32),
        compiler_params=_params(("parallel", "arbitrary"), 40),
        name="outproj",
    )(h, oa, ob, wa, wb)


def _layer(x, conv_buf, s_delta, attend, w, *, batch, seq, tm, gdn_tb, gdn_chunk):
    h, hn = _ffn(x, w["ffn1_norm"], w["ffn1_wg"], w["ffn1_wu"], w["ffn1_wd"], w["mix_norm"], tm=tm)
    ptm = min(2 * tm, x.shape[0])
    (ua,) = _proj(hn, w["w_qkvz"], "plain", tm=ptm, tn=512)
    (gates,) = _proj(hn, w["w_gate"], "plain", tm=ptm, tn=LANES)
    (qb,) = _proj(hn, w["w_qb"], "qnorm", w["q_norm"], tm=ptm, tn=512)
    kb32, kb16 = _proj(hn, w["w_kb"], "knorm", w["k_norm"], tm=ptm, tn=512)
    vb32, vb16 = _proj(hn, w["w_vb"], "dual", tm=ptm, tn=512)

    buf8 = jnp.pad(conv_buf, ((0, 0), (8 - (CONV_W - 1), 0), (0, 0)))
    oa, new_s = _gdn(ua, gates, buf8, w["conv_w"], w["A_log"], w["dt_bias"], w["gdn_out_norm"], s_delta,
                     batch=batch, seq=seq, tb=gdn_tb, chunk=gdn_chunk)
    assert seq >= CONV_W - 1
    new_buf = ua.reshape(batch, seq, 4 * D_A)[:, seq - (CONV_W - 1):, :3 * D_A]

    ob = attend(qb, kb16, vb16)
    h2 = _outproj(h, oa, ob, w["w_out_a"], w["w_out_b"], tm=ptm)
    y = _ffn(h2, w["ffn2_norm"], w["ffn2_wg"], w["ffn2_wu"], w["ffn2_wd"], tm=tm)
    return y, kb32, vb32, new_s, new_buf


def kernel(x_prompt, x_sample, cache_k, cache_v, page_table, state_delta, state_conv, ffn1_norm, ffn1_w_gate, ffn1_w_up, ffn1_w_down, mix_norm, w_in, conv_w, A_log, dt_bias, gdn_out_norm, q_norm, k_norm, lambda_q1, lambda_k1, lambda_q2, lambda_k2, diff_subln, w_out, ffn2_norm, ffn2_w_gate, ffn2_w_up, ffn2_w_down):
    depth = ffn1_norm.shape[0]
    bp, sp, _ = x_prompt.shape
    assert bp == 1
    bs, ss, _ = x_sample.shape
    n_pages = page_table.shape[1]
    n_pool = cache_k.shape[1]
    hw = 2 * DH
    width = H_B * hw

    yp = x_prompt.reshape(bp * sp, D_MODEL)
    ys = x_sample.reshape(bs * ss, D_MODEL)
    cache_k2 = cache_k.reshape(depth * n_pool, PAGE_SIZE, width)
    cache_v2 = cache_v.reshape(depth * n_pool, PAGE_SIZE, width)
    outs = [[] for _ in range(8)]
    for l in range(depth):
        lam_init = 0.8 - 0.6 * math.exp(-0.3 * l)
        wl = w_in[l]
        small = jnp.zeros((D_MODEL, LANES), F32).at[:, :2 * H_A].set(wl[:, 4 * D_A:4 * D_A + 2 * H_A])
        off = 4 * D_A + 2 * H_A
        w = {
            "ffn1_norm": ffn1_norm[l], "mix_norm": mix_norm[l], "ffn2_norm": ffn2_norm[l],
            "ffn1_wg": ffn1_w_gate[l].astype(BF16), "ffn1_wu": ffn1_w_up[l].astype(BF16),
            "ffn1_wd": ffn1_w_down[l].astype(BF16),
            "ffn2_wg": ffn2_w_gate[l].astype(BF16), "ffn2_wu": ffn2_w_up[l].astype(BF16),
            "ffn2_wd": ffn2_w_down[l].astype(BF16),
            "w_qkvz": wl[:, :4 * D_A].astype(BF16), "w_gate": small.astype(BF16),
            "w_qb": wl[:, off:off + D_B].astype(BF16), "w_kb": wl[:, off + D_B:off + 2 * D_B].astype(BF16),
            "w_vb": wl[:, off + 2 * D_B:off + 3 * D_B].astype(BF16),
            "w_out_a": w_out[l, :D_A].astype(BF16), "w_out_b": w_out[l, D_A:].astype(BF16),
            "conv_w": conv_w[l], "A_log": A_log[l], "dt_bias": dt_bias[l], "gdn_out_norm": gdn_out_norm[l],
            "q_norm": q_norm[l], "k_norm": k_norm[l],
        }
        lam_rows = jnp.zeros((8, DH), F32)
        lam_rows = lam_rows.at[0].set(lambda_q1[l]).at[1].set(lambda_k1[l]).at[2].set(lambda_q2[l]).at[3].set(lambda_k2[l])
        subln = diff_subln[l]

        def attend_prompt(q, k, v):
            return _attn_prompt(q, k, v, lam_rows, subln, lam_init, tq=512)

        def attend_sample(q, k, v):
            q4 = q.reshape(bs, ss, 2 * H_B, DH)
            eye = jnp.eye(2 * H_B, dtype=BF16)
            qbd = (q4[:, :, :, None, :] * eye[None, None, :, :, None])
            qbd = jnp.transpose(qbd, (0, 2, 1, 3, 4)).reshape(bs, 2 * H_B * ss, width)
            k_new = jnp.pad(k.reshape(bs, ss, width), ((0, 0), (0, 16 - ss), (0, 0)))
            v_new = jnp.pad(v.reshape(bs, ss, width), ((0, 0), (0, 16 - ss), (0, 0)))
            page_ids = (page_table + l * n_pool).reshape(-1).astype(jnp.int32)
            o = _attn_sample(qbd, k_new, v_new, cache_k2, cache_v2, page_ids, lam_rows, subln, lam_init,
                             n_pages=n_pages, group=4)
            return o.reshape(bs * ss, width)

        buf0 = jnp.zeros((bp, CONV_W - 1, 3 * D_A), F32)
        s0 = jnp.zeros((bp, H_A, DH, DH), F32)
        yp, kp, vp, s_p, b_p = _layer(yp, buf0, s0, attend_prompt, w, batch=bp, seq=sp, tm=512,
                                      gdn_tb=128, gdn_chunk=64)
        ys, k_s, v_s, s_s, b_s = _layer(ys, state_conv[l], state_delta[l], attend_sample, w, batch=bs, seq=ss,
                                        tm=bs * ss, gdn_tb=ss, gdn_chunk=ss)
        for lst, val in zip(outs, (kp.reshape(bp, sp, H_B, hw), vp.reshape(bp, sp, H_B, hw),
                                   k_s.reshape(bs, ss, H_B, hw), v_s.reshape(bs, ss, H_B, hw),
                                   s_p, b_p, s_s, b_s)):
            lst.append(val)
    return (yp.reshape(bp, sp, D_MODEL), ys.reshape(bs, ss, D_MODEL), *[jnp.stack(o) for o in outs])
```

```python
import functools
import math

import jax
import jax.numpy as jnp
from jax import lax
from jax.experimental import pallas as pl
from jax.experimental.pallas import tpu as pltpu

F32 = jnp.float32
BF16 = jnp.bfloat16

D_MODEL = 2048
D_A = 1024
D_B = 1024
DH = 128
H_A = D_A // DH
H_B = D_B // (2 * DH)
CONV_W = 4
D_FF = 5632
EPS = 1e-6
PAGE_SIZE = 128
LANES = 128
GATE_LANE = 8
TRI_BASE = 16

MIB = 1024 * 1024


def _params(semantics, vmem_mib):
    return pltpu.CompilerParams(dimension_semantics=semantics, vmem_limit_bytes=vmem_mib * MIB)


def _rms(x, g):
    return x * lax.rsqrt(jnp.mean(x * x, axis=-1, keepdims=True) + EPS) * g


def _silu(x):
    return x * jax.nn.sigmoid(x)


def _softplus(x):
    return jnp.maximum(x, 0.0) + jnp.log1p(jnp.exp(-jnp.abs(x)))


def _ffn_body(x_ref, g_ref, wg_ref, wu_ref, wd_ref, *rest, nf, with_norm):
    if with_norm:
        ng_ref, o_ref, on_ref, xn_ref = rest
    else:
        o_ref, xn_ref = rest
    f = pl.program_id(1)

    @pl.when(f == 0)
    def _():
        xn_ref[...] = _rms(x_ref[...], g_ref[...]).astype(BF16)
        o_ref[...] = jnp.zeros_like(o_ref)

    xn = xn_ref[...]
    a = jnp.dot(xn, wg_ref[...], preferred_element_type=F32)
    b = jnp.dot(xn, wu_ref[...], preferred_element_type=F32)
    hid = (_silu(a) * b).astype(BF16)
    o_ref[...] += jnp.dot(hid, wd_ref[...], preferred_element_type=F32)

    @pl.when(f == nf - 1)
    def _():
        y = x_ref[...] + 0.5 * o_ref[...]
        o_ref[...] = y
        if with_norm:
            on_ref[...] = _rms(y, ng_ref[...]).astype(BF16)


def _ffn(x, g, wg, wu, wd, next_g=None, *, tm, tf=512):
    t, d = x.shape
    nf = D_FF // tf
    with_norm = next_g is not None
    in_specs = [
        pl.BlockSpec((tm, d), lambda i, f: (i, 0)),
        pl.BlockSpec((1, d), lambda i, f: (0, 0)),
        pl.BlockSpec((d, tf), lambda i, f: (0, f)),
        pl.BlockSpec((d, tf), lambda i, f: (0, f)),
        pl.BlockSpec((tf, d), lambda i, f: (f, 0)),
    ]
    args = [x, g.reshape(1, d), wg, wu, wd]
    out_shape = [jax.ShapeDtypeStruct((t, d), F32)]
    out_specs = [pl.BlockSpec((tm, d), lambda i, f: (i, 0))]
    if with_norm:
        in_specs.append(pl.BlockSpec((1, d), lambda i, f: (0, 0)))
        args.append(next_g.reshape(1, d))
        out_shape.append(jax.ShapeDtypeStruct((t, d), BF16))
        out_specs.append(pl.BlockSpec((tm, d), lambda i, f: (i, 0)))
    res = pl.pallas_call(
        functools.partial(_ffn_body, nf=nf, with_norm=with_norm),
        grid=(t // tm, nf),
        in_specs=in_specs,
        out_specs=out_specs,
        out_shape=out_shape,
        scratch_shapes=[pltpu.VMEM((tm, d), BF16)],
        compiler_params=_params(("parallel", "arbitrary"), 48),
        name="ffn",
    )(*args)
    return res if with_norm else res[0]


def _head_norm_store(acc, g_ref, scale, refs):
    for c in range(acc.shape[1] // DH):
        seg = acc[:, c * DH:(c + 1) * DH]
        y = _rms(seg, g_ref[...])
        if scale != 1.0:
            y = y * scale
        for r in refs:
            r[:, c * DH:(c + 1) * DH] = y.astype(r.dtype)


def _proj_body(x_ref, w_ref, *rest, mode):
    acc = jnp.dot(x_ref[...], w_ref[...], preferred_element_type=F32)
    if mode == "plain":
        (o_ref,) = rest
        o_ref[...] = acc
    elif mode == "qnorm":
        g_ref, o_ref = rest
        _head_norm_store(acc, g_ref, DH ** -0.5, (o_ref,))
    elif mode == "knorm":
        g_ref, o32_ref, o16_ref = rest
        _head_norm_store(acc, g_ref, 1.0, (o32_ref, o16_ref))
    elif mode == "dual":
        o32_ref, o16_ref = rest
        o32_ref[...] = acc
        o16_ref[...] = acc.astype(BF16)
    else:
        o32_ref, o16t_ref = rest
        o32_ref[...] = acc
        tk = o16t_ref.shape[2]
        for r in range(o16t_ref.shape[0]):
            o16t_ref[r] = acc[r * tk:(r + 1) * tk, :].T.astype(BF16)


def _proj(xn, w, mode, gain=None, *, tm, tn, tk=None):
    t, d = xn.shape
    n = w.shape[1]
    in_specs = [pl.BlockSpec((tm, d), lambda i, j: (i, 0)), pl.BlockSpec((d, tn), lambda i, j: (0, j))]
    args = [xn, w]
    if mode in ("qnorm", "knorm"):
        in_specs.append(pl.BlockSpec((1, DH), lambda i, j: (0, 0)))
        args.append(gain.reshape(1, DH))
    o_spec = pl.BlockSpec((tm, tn), lambda i, j: (i, j))
    if mode == "plain":
        out_shape, out_specs = [jax.ShapeDtypeStruct((t, n), F32)], [o_spec]
    elif mode == "qnorm":
        out_shape, out_specs = [jax.ShapeDtypeStruct((t, n), BF16)], [o_spec]
    elif mode == "dual_t":
        out_shape = [jax.ShapeDtypeStruct((t, n), F32), jax.ShapeDtypeStruct((t // tk, n, tk), BF16)]
        out_specs = [o_spec, pl.BlockSpec((tm // tk, tn, tk), lambda i, j: (i, j, 0))]
    else:
        out_shape = [jax.ShapeDtypeStruct((t, n), F32), jax.ShapeDtypeStruct((t, n), BF16)]
        out_specs = [o_spec, o_spec]
    return pl.pallas_call(
        functools.partial(_proj_body, mode=mode),
        grid=(t // tm, n // tn),
        in_specs=in_specs,
        out_specs=out_specs,
        out_shape=out_shape,
        compiler_params=_params(("parallel", "arbitrary"), 40),
        name="proj_" + mode,
    )(*args)


def _split(x):
    hi = x.astype(BF16)
    lo = (x - hi.astype(F32)).astype(BF16)
    return hi, lo


def _dot3(a, b, dims=(((1,), (0,)), ((), ()))):
    ah, al = _split(a)
    bh, bl = _split(b)
    dg = functools.partial(lax.dot_general, dimension_numbers=dims, preferred_element_type=F32)
    return dg(ah, bh) + (dg(ah, bl) + dg(al, bh))


_NT = (((1,), (1,)), ((), ()))
_TN = (((0,), (0,)), ((), ()))


_B_NN = (((2,), (1,)), ((0,), (0,)))
_B_NT = (((2,), (2,)), ((0,), (0,)))
_B_TN = (((1,), (1,)), ((0,), (0,)))


def _bdot(a, b, dims=_B_NN):
    return lax.dot_general(a.astype(BF16), b.astype(BF16), dims, preferred_element_type=F32)


def _tri_inv(a):
    c = a.shape[-1]
    base = min(TRI_BASE, c)
    row = lax.broadcasted_iota(jnp.int32, (c, c), 0)
    col = lax.broadcasted_iota(jnp.int32, (c, c), 1)
    eye = (row == col).astype(F32)

    def neumann(m, order):
        x = eye - m
        p = m
        k = 1
        while 2 * k < order:
            p = _bdot(p, p)
            x = x + _bdot(x, p)
            k *= 2
        return x

    if c == base:
        return neumann(a, base)
    shift = base.bit_length() - 1
    same = (row >> shift) == (col >> shift)
    dinv = neumann(jnp.where(same, a, 0.0), base)
    e = _bdot(dinv, jnp.where(same, 0.0, a))
    return _bdot(neumann(e, c // base), dinv)


def _gdn_body(qkv_ref, z_ref, gt_ref, buf_ref, cw_ref, ad_ref, gn_ref, s0_ref,
              o_ref, s_ref, xp_ref, *, tb):
    t = pl.program_id(1)

    @pl.when(t == 0)
    def _():
        s_ref[...] = s0_ref[...]
        xp_ref[0:8, :] = buf_ref[0]

    xp_ref[8:8 + tb, :] = qkv_ref[...]
    y = cw_ref[0:1, :] * xp_ref[5:5 + tb, :]
    y = y + cw_ref[1:2, :] * xp_ref[6:6 + tb, :]
    y = y + cw_ref[2:3, :] * xp_ref[7:7 + tb, :]
    y = _silu(y + cw_ref[3:4, :] * xp_ref[8:8 + tb, :])
    xp_ref[0:8, :] = xp_ref[tb:tb + 8, :]

    def heads(base):
        return jnp.stack([y[:, base + h * DH:base + (h + 1) * DH] for h in range(H_A)])

    def l2(x):
        return x * lax.rsqrt(jnp.sum(x * x, axis=-1, keepdims=True) + EPS)

    q = l2(heads(0)) * (DH ** -0.5)
    k = l2(heads(D_A))
    v = heads(2 * D_A)

    gt = gt_ref[...]
    beta = jax.nn.sigmoid(gt)
    g = -jnp.exp(ad_ref[0:1, :]) * _softplus(gt + ad_ref[1:2, :])
    row = lax.broadcasted_iota(jnp.int32, (tb, tb), 0)
    col = lax.broadcasted_iota(jnp.int32, (tb, tb), 1)
    tri_incl = col <= row
    tri_strict = col < row
    gc = _dot3(tri_incl.astype(F32), g)
    pad = max(LANES - tb, 0)
    gc_rows = jnp.concatenate([gc, jnp.zeros((pad, LANES), F32)], axis=0) if pad else gc
    gct = gc_rows.T

    bcol = jnp.stack([beta[:, h:h + 1] for h in range(H_A)])
    gcol = jnp.stack([gc[:, GATE_LANE + h:GATE_LANE + h + 1] for h in range(H_A)])
    grow = jnp.stack([gct[GATE_LANE + h:GATE_LANE + h + 1, 0:tb] for h in range(H_A)])
    decay = jnp.where(tri_incl, jnp.exp(jnp.where(tri_incl, gcol - grow, 0.0)), 0.0)
    egc = jnp.exp(gcol)
    kb = k * bcol

    amat = jnp.where(tri_strict, _bdot(kb, k, _B_NT) * decay, 0.0)
    tinv = _tri_inv(amat)
    uw = _bdot(tinv, jnp.concatenate([v * bcol, kb * egc], axis=-1))
    u, w = uw[:, :, :DH], uw[:, :, DH:]
    qk = jnp.where(tri_incl, _bdot(q, k, _B_NT) * decay, 0.0)

    s = s_ref[0]
    wq_s = _bdot(jnp.concatenate([w, q * egc], axis=1), s)
    v_new = u - wq_s[:, :tb]
    o = wq_s[:, tb:] + _bdot(qk, v_new)
    g_last = gcol[:, tb - 1:tb, :]
    k_dec = k * jnp.exp(g_last - gcol)
    s_ref[0] = s * jnp.exp(g_last) + _bdot(k_dec, v_new, _B_TN)

    on = _rms(o, gn_ref[...])
    for h in range(H_A):
        gate = _silu(z_ref[:, h * DH:(h + 1) * DH])
        o_ref[:, h * DH:(h + 1) * DH] = (on[h] * gate).astype(BF16)


def _gdn(ua, gates, buf8, conv_w, a_log, dt_bias, out_norm, s0, *, batch, seq, tb):
    nt = seq // tb
    ad = jnp.zeros((8, LANES), F32)
    ad = ad.at[0, GATE_LANE:GATE_LANE + H_A].set(a_log).at[1, GATE_LANE:GATE_LANE + H_A].set(dt_bias)
    cw = jnp.zeros((8, 3 * D_A), F32).at[:CONV_W].set(conv_w)
    return pl.pallas_call(
        functools.partial(_gdn_body, tb=tb),
        grid=(batch, nt),
        in_specs=[
            pl.BlockSpec((tb, 3 * D_A), lambda b, t: (b * nt + t, 0)),
            pl.BlockSpec((tb, D_A), lambda b, t: (b * nt + t, 3)),
            pl.BlockSpec((tb, LANES), lambda b, t: (b * nt + t, 0)),
            pl.BlockSpec((1, 8, 3 * D_A), lambda b, t: (b, 0, 0)),
            pl.BlockSpec((8, 3 * D_A), lambda b, t: (0, 0)),
            pl.BlockSpec((8, LANES), lambda b, t: (0, 0)),
            pl.BlockSpec((1, DH), lambda b, t: (0, 0)),
            pl.BlockSpec((1, H_A, DH, DH), lambda b, t: (b, 0, 0, 0)),
        ],
        out_specs=[
            pl.BlockSpec((tb, D_A), lambda b, t: (b * nt + t, 0)),
            pl.BlockSpec((1, H_A, DH, DH), lambda b, t: (b, 0, 0, 0)),
        ],
        out_shape=[
            jax.ShapeDtypeStruct((batch * seq, D_A), BF16),
            jax.ShapeDtypeStruct((batch, H_A, DH, DH), F32),
        ],
        scratch_shapes=[pltpu.VMEM((tb + 8, 3 * D_A), F32)],
        compiler_params=_params(("parallel", "arbitrary"), 40),
        name="gdn",
    )(ua, ua, gates, buf8, cw, ad, out_norm.reshape(1, DH), s0)


def _lambda(lam_ref, lam_init):
    l1 = jnp.sum(lam_ref[0:1, :] * lam_ref[1:2, :], axis=-1, keepdims=True)
    l2 = jnp.sum(lam_ref[2:3, :] * lam_ref[3:4, :], axis=-1, keepdims=True)
    return jnp.exp(l1) - jnp.exp(l2) + lam_init


def _attn_prompt_body(lam_ref, sg_ref, q_ref, k_ref, vt_ref, o_ref, acc_ref, *, tq, lam_init):
    qi = pl.program_id(1)
    acc_ref[...] = jnp.zeros(acc_ref.shape, F32)

    def block(j, stats, masked):
        ks = k_ref[pl.ds(pl.multiple_of(j * tq, tq), tq), :]
        vt = vt_ref[j]
        new_stats = []
        for sub in range(2):
            cols = slice(sub * DH, (sub + 1) * DH)
            st = lax.dot_general(ks[:, cols], q_ref[:, cols], _NT, preferred_element_type=F32)
            if masked:
                key = lax.broadcasted_iota(jnp.int32, (tq, tq), 0)
                qry = lax.broadcasted_iota(jnp.int32, (tq, tq), 1)
                st = jnp.where(key <= qry, st, -jnp.inf)
            m_prev, l_prev = stats[sub]
            m_new = jnp.maximum(m_prev, jnp.max(st, axis=0, keepdims=True))
            corr = jnp.exp(m_prev - m_new)
            p = jnp.exp(st - m_new)
            l_new = corr * l_prev + jnp.sum(p, axis=0, keepdims=True)
            acc_ref[sub] = acc_ref[sub] * corr + jnp.dot(vt, p.astype(BF16), preferred_element_type=F32)
            new_stats.append((m_new, l_new))
        return tuple(new_stats)

    init = (jnp.full((1, tq), -jnp.inf, F32), jnp.zeros((1, tq), F32))
    stats = lax.fori_loop(0, qi, lambda j, st: block(j, st, False), (init, init))
    (_, l0), (_, l1) = block(qi, stats, True)

    lam = _lambda(lam_ref, lam_init)
    ot = acc_ref[0] / l0 - lam * (acc_ref[1] / l1)
    o_ref[...] = (_rms(ot.T, sg_ref[...]) * (1.0 - lam_init)).astype(BF16)


def _attn_prompt(q, k, vt, lam_rows, subln, lam_init, *, tq):
    s_len = q.shape[0]
    hw = 2 * DH
    return pl.pallas_call(
        functools.partial(_attn_prompt_body, tq=tq, lam_init=lam_init),
        grid=(H_B, s_len // tq),
        in_specs=[
            pl.BlockSpec((8, DH), lambda h, i: (0, 0)),
            pl.BlockSpec((1, hw), lambda h, i: (0, 0)),
            pl.BlockSpec((tq, hw), lambda h, i: (i, h)),
            pl.BlockSpec((s_len, hw), lambda h, i: (0, h)),
            pl.BlockSpec((s_len // tq, hw, tq), lambda h, i: (0, h, 0)),
        ],
        out_specs=pl.BlockSpec((tq, hw), lambda h, i: (i, h)),
        out_shape=jax.ShapeDtypeStruct((s_len, H_B * hw), BF16),
        scratch_shapes=[pltpu.VMEM((2, hw, tq), F32)],
        compiler_params=_params(("parallel", "arbitrary"), 52),
        name="attn_prompt",
    )(lam_rows, subln.reshape(1, hw), q, k, vt)


def _attn_sample_body(pt_ref, lam_ref, sg_ref, q_ref, kn_ref, vn_ref, *rest, group, n_steps, n_new, lam_init):
    kc_refs = rest[:group]
    vc_refs = rest[group:2 * group]
    o_ref, m_ref, l_ref, acc_ref = rest[2 * group:]
    step = pl.program_id(1)
    rows = q_ref.shape[1]
    q = q_ref[0]

    def update(s, vals):
        m_prev = m_ref[:, 0:1]
        m_new = jnp.maximum(m_prev, jnp.max(s, axis=-1, keepdims=True))
        corr = jnp.exp(m_prev - m_new)
        p = jnp.exp(s - m_new)
        l_new = corr * l_ref[:, 0:1] + jnp.sum(p, axis=-1, keepdims=True)
        pb = p.astype(BF16)
        pv = None
        width = s.shape[1] // len(vals)
        for i, val in enumerate(vals):
            part = jnp.dot(pb[:, i * width:(i + 1) * width], val, preferred_element_type=F32)
            pv = part if pv is None else pv + part
        acc_ref[...] = acc_ref[...] * corr + pv
        m_ref[...] = jnp.broadcast_to(m_new, m_ref.shape)
        l_ref[...] = jnp.broadcast_to(l_new, l_ref.shape)

    @pl.when(step == 0)
    def _():
        m_ref[...] = jnp.full(m_ref.shape, -jnp.inf, F32)
        l_ref[...] = jnp.zeros(l_ref.shape, F32)
        acc_ref[...] = jnp.zeros(acc_ref.shape, F32)
        kn = kn_ref[0]
        s = lax.dot_general(q, kn, _NT, preferred_element_type=F32)
        tok = lax.broadcasted_iota(jnp.int32, s.shape, 0) & (n_new - 1)
        key = lax.broadcasted_iota(jnp.int32, s.shape, 1)
        s = jnp.where(key <= tok, s, -jnp.inf)
        update(s, [vn_ref[0]])

    def page(ref):
        parts = [ref[pl.ds(half * H_B + h, PAGE_SIZE, stride=2 * H_B), :] for h in range(H_B) for half in range(2)]
        return jnp.concatenate(parts, axis=1).astype(BF16)

    s_parts = [lax.dot_general(q, page(kc), _NT, preferred_element_type=F32) for kc in kc_refs]
    update(jnp.concatenate(s_parts, axis=-1), [page(vc) for vc in vc_refs])

    @pl.when(step == n_steps - 1)
    def _():
        lam = _lambda(lam_ref, lam_init)
        hw = 2 * DH
        for h in range(H_B):
            r0 = h * 2 * n_new
            a0 = acc_ref[r0:r0 + n_new, h * hw:(h + 1) * hw] / l_ref[r0:r0 + n_new, 0:1]
            a1 = acc_ref[r0 + n_new:r0 + 2 * n_new, h * hw:(h + 1) * hw] / l_ref[r0 + n_new:r0 + 2 * n_new, 0:1]
            o = a0 - lam * a1
            o_ref[0, :, h * hw:(h + 1) * hw] = (_rms(o, sg_ref[...]) * (1.0 - lam_init)).astype(BF16)


def _attn_sample(qbd, k_new, v_new, cache_k, cache_v, page_ids, lam_rows, subln, lam_init, *, n_pages, group):
    batch, rows, width = qbd.shape
    n_new = rows // (2 * H_B)
    n_steps = n_pages // group
    hw = 2 * DH
    page_rows = PAGE_SIZE * 2 * H_B

    def cache_spec(g):
        return pl.BlockSpec((page_rows, DH), lambda b, s, pt: (pt[b * n_pages + s * group + g], 0))

    in_specs = [
        pl.BlockSpec((8, DH), lambda b, s, pt: (0, 0)),
        pl.BlockSpec((1, hw), lambda b, s, pt: (0, 0)),
        pl.BlockSpec((1, rows, width), lambda b, s, pt: (b, 0, 0)),
        pl.BlockSpec((1, k_new.shape[1], width), lambda b, s, pt: (b, 0, 0)),
        pl.BlockSpec((1, v_new.shape[1], width), lambda b, s, pt: (b, 0, 0)),
    ] + [cache_spec(g) for g in range(group)] + [cache_spec(g) for g in range(group)]
    grid_spec = pltpu.PrefetchScalarGridSpec(
        num_scalar_prefetch=1,
        grid=(batch, n_steps),
        in_specs=in_specs,
        out_specs=pl.BlockSpec((1, n_new, width), lambda b, s, pt: (b, 0, 0)),
        scratch_shapes=[
            pltpu.VMEM((rows, LANES), F32),
            pltpu.VMEM((rows, LANES), F32),
            pltpu.VMEM((rows, width), F32),
        ],
    )
    return pl.pallas_call(
        functools.partial(_attn_sample_body, group=group, n_steps=n_steps, n_new=n_new, lam_init=lam_init),
        grid_spec=grid_spec,
        out_shape=jax.ShapeDtypeStruct((batch, n_new, width), BF16),
        compiler_params=_params(("parallel", "arbitrary"), 40),
        name="attn_sample",
    )(page_ids, lam_rows, subln.reshape(1, hw), qbd, k_new, v_new,
      *([cache_k] * group), *([cache_v] * group))


def _outproj_body(h_ref, a_ref, b_ref, wa_ref, wb_ref, o_ref):
    mix = jnp.dot(a_ref[...], wa_ref[...], preferred_element_type=F32)
    mix = mix + jnp.dot(b_ref[...], wb_ref[...], preferred_element_type=F32)
    o_ref[...] = h_ref[...] + mix


def _outproj(h, oa, ob, wa, wb, *, tm, tn=512):
    t, d = h.shape
    return pl.pallas_call(
        _outproj_body,
        grid=(t // tm, d // tn),
        in_specs=[
            pl.BlockSpec((tm, tn), lambda i, j: (i, j)),
            pl.BlockSpec((tm, D_A), lambda i, j: (i, 0)),
            pl.BlockSpec((tm, D_B), lambda i, j: (i, 0)),
            pl.BlockSpec((D_A, tn), lambda i, j: (0, j)),
            pl.BlockSpec((D_B, tn), lambda i, j: (0, j)),
        ],
        out_specs=pl.BlockSpec((tm, tn), lambda i, j: (i, j)),
        out_shape=jax.ShapeDtypeStruct((t, d), F32),
        compiler_params=_params(("parallel", "arbitrary"), 40),
        name="outproj",
    )(h, oa, ob, wa, wb)


def _layer(x, conv_buf, s_delta, attend, w, *, batch, seq, tm, gdn_tb, v_block=None):
    h, hn = _ffn(x, w["ffn1_norm"], w["ffn1_wg"], w["ffn1_wu"], w["ffn1_wd"], w["mix_norm"], tm=tm)
    ptm = min(2 * tm, x.shape[0])
    (ua,) = _proj(hn, w["w_qkvz"], "plain", tm=ptm, tn=512)
    (gates,) = _proj(hn, w["w_gate"], "plain", tm=ptm, tn=LANES)
    (qb,) = _proj(hn, w["w_qb"], "qnorm", w["q_norm"], tm=ptm, tn=512)
    kb32, kb16 = _proj(hn, w["w_kb"], "knorm", w["k_norm"], tm=ptm, tn=512)
    if v_block is None:
        vb32, vb16 = _proj(hn, w["w_vb"], "dual", tm=ptm, tn=512)
    else:
        vb32, vb16 = _proj(hn, w["w_vb"], "dual_t", tm=ptm, tn=512, tk=v_block)

    buf8 = jnp.pad(conv_buf, ((0, 0), (8 - (CONV_W - 1), 0), (0, 0)))
    oa, new_s = _gdn(ua, gates, buf8, w["conv_w"], w["A_log"], w["dt_bias"], w["gdn_out_norm"], s_delta,
                     batch=batch, seq=seq, tb=gdn_tb)
    assert seq >= CONV_W - 1
    new_buf = ua.reshape(batch, seq, 4 * D_A)[:, seq - (CONV_W - 1):, :3 * D_A]

    ob = attend(qb, kb16, vb16)
    h2 = _outproj(h, oa, ob, w["w_out_a"], w["w_out_b"], tm=ptm)
    y = _ffn(h2, w["ffn2_norm"], w["ffn2_wg"], w["ffn2_wu"], w["ffn2_wd"], tm=tm)
    return y, kb32, vb32, new_s, new_buf


def kernel(x_prompt, x_sample, cache_k, cache_v, page_table, state_delta, state_conv, ffn1_norm, ffn1_w_gate, ffn1_w_up, ffn1_w_down, mix_norm, w_in, conv_w, A_log, dt_bias, gdn_out_norm, q_norm, k_norm, lambda_q1, lambda_k1, lambda_q2, lambda_k2, diff_subln, w_out, ffn2_norm, ffn2_w_gate, ffn2_w_up, ffn2_w_down):
    depth = ffn1_norm.shape[0]
    bp, sp, _ = x_prompt.shape
    assert bp == 1
    bs, ss, _ = x_sample.shape
    n_pages = page_table.shape[1]
    n_pool = cache_k.shape[1]
    hw = 2 * DH
    width = H_B * hw

    yp = x_prompt.reshape(bp * sp, D_MODEL)
    ys = x_sample.reshape(bs * ss, D_MODEL)
    def page_rows(cache):
        x = cache.reshape(depth * n_pool, PAGE_SIZE, H_B, 2, DH)
        return jnp.transpose(x, (0, 1, 3, 2, 4)).reshape(depth * n_pool * PAGE_SIZE * 2 * H_B, DH)

    cache_k2 = page_rows(cache_k)
    cache_v2 = page_rows(cache_v)
    outs = [[] for _ in range(8)]
    for l in range(depth):
        lam_init = 0.8 - 0.6 * math.exp(-0.3 * l)
        wl = w_in[l]
        small = jnp.zeros((D_MODEL, LANES), F32).at[:, :2 * H_A].set(wl[:, 4 * D_A:4 * D_A + 2 * H_A])
        off = 4 * D_A + 2 * H_A
        w = {
            "ffn1_norm": ffn1_norm[l], "mix_norm": mix_norm[l], "ffn2_norm": ffn2_norm[l],
            "ffn1_wg": ffn1_w_gate[l].astype(BF16), "ffn1_wu": ffn1_w_up[l].astype(BF16),
            "ffn1_wd": ffn1_w_down[l].astype(BF16),
            "ffn2_wg": ffn2_w_gate[l].astype(BF16), "ffn2_wu": ffn2_w_up[l].astype(BF16),
            "ffn2_wd": ffn2_w_down[l].astype(BF16),
            "w_qkvz": wl[:, :4 * D_A].astype(BF16), "w_gate": small.astype(BF16),
            "w_qb": wl[:, off:off + D_B].astype(BF16), "w_kb": wl[:, off + D_B:off + 2 * D_B].astype(BF16),
            "w_vb": wl[:, off + 2 * D_B:off + 3 * D_B].astype(BF16),
            "w_out_a": w_out[l, :D_A].astype(BF16), "w_out_b": w_out[l, D_A:].astype(BF16),
            "conv_w": conv_w[l], "A_log": A_log[l], "dt_bias": dt_bias[l], "gdn_out_norm": gdn_out_norm[l],
            "q_norm": q_norm[l], "k_norm": k_norm[l],
        }
        lam_rows = jnp.zeros((8, DH), F32)
        lam_rows = lam_rows.at[0].set(lambda_q1[l]).at[1].set(lambda_k1[l]).at[2].set(lambda_q2[l]).at[3].set(lambda_k2[l])
        subln = diff_subln[l]

        tq = 512

        def attend_prompt(q, k, vt):
            return _attn_prompt(q, k, vt, lam_rows, subln, lam_init, tq=tq)

        def attend_sample(q, k, v):
            q4 = q.reshape(bs, ss, 2 * H_B, DH)
            eye = jnp.eye(2 * H_B, dtype=BF16)
            qbd = (q4[:, :, :, None, :] * eye[None, None, :, :, None])
            qbd = jnp.transpose(qbd, (0, 2, 1, 3, 4)).reshape(bs, 2 * H_B * ss, width)
            k_new = jnp.pad(k.reshape(bs, ss, width), ((0, 0), (0, 16 - ss), (0, 0)))
            v_new = jnp.pad(v.reshape(bs, ss, width), ((0, 0), (0, 16 - ss), (0, 0)))
            page_ids = (page_table + l * n_pool).reshape(-1).astype(jnp.int32)
            o = _attn_sample(qbd, k_new, v_new, cache_k2, cache_v2, page_ids, lam_rows, subln, lam_init,
                             n_pages=n_pages, group=4)
            return o.reshape(bs * ss, width)

        buf0 = jnp.zeros((bp, CONV_W - 1, 3 * D_A), F32)
        s0 = jnp.zeros((bp, H_A, DH, DH), F32)
        yp, kp, vp, s_p, b_p = _layer(yp, buf0, s0, attend_prompt, w, batch=bp, seq=sp, tm=512,
                                      gdn_tb=128, v_block=tq)
        ys, k_s, v_s, s_s, b_s = _layer(ys, state_conv[l], state_delta[l], attend_sample, w, batch=bs, seq=ss,
                                        tm=bs * ss, gdn_tb=ss)
        for lst, val in zip(outs, (kp.reshape(bp, sp, H_B, hw), vp.reshape(bp, sp, H_B, hw),
                                   k_s.reshape(bs, ss, H_B, hw), v_s.reshape(bs, ss, H_B, hw),
                                   s_p, b_p, s_s, b_s)):
            lst.append(val)
    return (yp.reshape(bp, sp, D_MODEL), ys.reshape(bs, ss, D_MODEL), *[jnp.stack(o) for o in outs])
```

```python
import functools
import math

import jax
import jax.numpy as jnp
from jax import lax
from jax.experimental import pallas as pl
from jax.experimental.pallas import tpu as pltpu

F32 = jnp.float32
BF16 = jnp.bfloat16

D_MODEL = 2048
D_A = 1024
D_B = 1024
DH = 128
H_A = D_A // DH
H_B = D_B // (2 * DH)
CONV_W = 4
D_FF = 5632
EPS = 1e-6
PAGE_SIZE = 128
LANES = 128
BF16_ROWS = 16
LOG2E = math.log2(math.e)
GATE_LANE = 8
TRI_BASE = 16

MIB = 1024 * 1024


def _params(semantics, vmem_mib):
    return pltpu.CompilerParams(dimension_semantics=semantics, vmem_limit_bytes=vmem_mib * MIB)


def _rms(x, g):
    return x * lax.rsqrt(jnp.mean(x * x, axis=-1, keepdims=True) + EPS) * g


def _silu(x):
    return x * jax.nn.sigmoid(x)


def _softplus(x):
    return jnp.maximum(x, 0.0) + jnp.log1p(jnp.exp(-jnp.abs(x)))


def _ffn_body(x_ref, g_ref, wg_ref, wu_ref, wd_ref, *rest, nf, with_norm):
    if with_norm:
        ng_ref, o_ref, on_ref, xn_ref = rest
    else:
        o_ref, xn_ref = rest
    f = pl.program_id(1)

    @pl.when(f == 0)
    def _():
        xn_ref[...] = _rms(x_ref[...], g_ref[...]).astype(BF16)
        o_ref[...] = jnp.zeros_like(o_ref)

    xn = xn_ref[...]
    a = jnp.dot(xn, wg_ref[...], preferred_element_type=F32)
    b = jnp.dot(xn, wu_ref[...], preferred_element_type=F32)
    hid = (_silu(a) * b).astype(BF16)
    o_ref[...] += jnp.dot(hid, wd_ref[...], preferred_element_type=F32)

    @pl.when(f == nf - 1)
    def _():
        y = x_ref[...] + 0.5 * o_ref[...]
        o_ref[...] = y
        if with_norm:
            on_ref[...] = _rms(y, ng_ref[...]).astype(BF16)


def _ffn(x, g, wg, wu, wd, next_g=None, *, tm, tf=512):
    t, d = x.shape
    nf = D_FF // tf
    with_norm = next_g is not None
    in_specs = [
        pl.BlockSpec((tm, d), lambda i, f: (i, 0)),
        pl.BlockSpec((1, d), lambda i, f: (0, 0)),
        pl.BlockSpec((d, tf), lambda i, f: (0, f)),
        pl.BlockSpec((d, tf), lambda i, f: (0, f)),
        pl.BlockSpec((tf, d), lambda i, f: (f, 0)),
    ]
    args = [x, g.reshape(1, d), wg, wu, wd]
    out_shape = [jax.ShapeDtypeStruct((t, d), F32)]
    out_specs = [pl.BlockSpec((tm, d), lambda i, f: (i, 0))]
    if with_norm:
        in_specs.append(pl.BlockSpec((1, d), lambda i, f: (0, 0)))
        args.append(next_g.reshape(1, d))
        out_shape.append(jax.ShapeDtypeStruct((t, d), BF16))
        out_specs.append(pl.BlockSpec((tm, d), lambda i, f: (i, 0)))
    res = pl.pallas_call(
        functools.partial(_ffn_body, nf=nf, with_norm=with_norm),
        grid=(t // tm, nf),
        in_specs=in_specs,
        out_specs=out_specs,
        out_shape=out_shape,
        scratch_shapes=[pltpu.VMEM((tm, d), BF16)],
        compiler_params=_params(("parallel", "arbitrary"), 48),
        name="ffn",
    )(*args)
    return res if with_norm else res[0]


def _head_norm_store(acc, g_ref, scale, refs):
    for c in range(acc.shape[1] // DH):
        seg = acc[:, c * DH:(c + 1) * DH]
        y = _rms(seg, g_ref[...])
        if scale != 1.0:
            y = y * scale
        for r in refs:
            r[:, c * DH:(c + 1) * DH] = y.astype(r.dtype)


def _proj_body(x_ref, w_ref, *rest, mode):
    acc = jnp.dot(x_ref[...], w_ref[...], preferred_element_type=F32)
    if mode == "plain":
        (o_ref,) = rest
        o_ref[...] = acc
    elif mode == "qnorm":
        g_ref, o_ref = rest
        _head_norm_store(acc, g_ref, LOG2E * DH ** -0.5, (o_ref,))
    elif mode == "knorm":
        g_ref, o32_ref, o16_ref = rest
        _head_norm_store(acc, g_ref, 1.0, (o32_ref, o16_ref))
    elif mode == "dual":
        o32_ref, o16_ref = rest
        o32_ref[...] = acc
        o16_ref[...] = acc.astype(BF16)
    else:
        o32_ref, o16t_ref = rest
        o32_ref[...] = acc
        tk = o16t_ref.shape[2]
        for r in range(o16t_ref.shape[0]):
            o16t_ref[r] = acc[r * tk:(r + 1) * tk, :].T.astype(BF16)


def _proj(xn, w, mode, gain=None, *, tm, tn, tk=None):
    t, d = xn.shape
    n = w.shape[1]
    in_specs = [pl.BlockSpec((tm, d), lambda i, j: (i, 0)), pl.BlockSpec((d, tn), lambda i, j: (0, j))]
    args = [xn, w]
    if mode in ("qnorm", "knorm"):
        in_specs.append(pl.BlockSpec((1, DH), lambda i, j: (0, 0)))
        args.append(gain.reshape(1, DH))
    o_spec = pl.BlockSpec((tm, tn), lambda i, j: (i, j))
    if mode == "plain":
        out_shape, out_specs = [jax.ShapeDtypeStruct((t, n), F32)], [o_spec]
    elif mode == "qnorm":
        out_shape, out_specs = [jax.ShapeDtypeStruct((t, n), BF16)], [o_spec]
    elif mode == "dual_t":
        out_shape = [jax.ShapeDtypeStruct((t, n), F32), jax.ShapeDtypeStruct((t // tk, n, tk), BF16)]
        out_specs = [o_spec, pl.BlockSpec((tm // tk, tn, tk), lambda i, j: (i, j, 0))]
    else:
        out_shape = [jax.ShapeDtypeStruct((t, n), F32), jax.ShapeDtypeStruct((t, n), BF16)]
        out_specs = [o_spec, o_spec]
    return pl.pallas_call(
        functools.partial(_proj_body, mode=mode),
        grid=(t // tm, n // tn),
        in_specs=in_specs,
        out_specs=out_specs,
        out_shape=out_shape,
        compiler_params=_params(("parallel", "arbitrary"), 40),
        name="proj_" + mode,
    )(*args)


def _split(x):
    hi = x.astype(BF16)
    lo = (x - hi.astype(F32)).astype(BF16)
    return hi, lo


def _dot3(a, b, dims=(((1,), (0,)), ((), ()))):
    ah, al = _split(a)
    bh, bl = _split(b)
    dg = functools.partial(lax.dot_general, dimension_numbers=dims, preferred_element_type=F32)
    return dg(ah, bh) + (dg(ah, bl) + dg(al, bh))


_NT = (((1,), (1,)), ((), ()))
_TN = (((0,), (0,)), ((), ()))


_B_NN = (((2,), (1,)), ((0,), (0,)))
_B_NT = (((2,), (2,)), ((0,), (0,)))
_B_TN = (((1,), (1,)), ((0,), (0,)))


def _bdot(a, b, dims=_B_NN):
    return lax.dot_general(a.astype(BF16), b.astype(BF16), dims, preferred_element_type=F32)


def _tri_inv(a):
    c = a.shape[-1]
    base = min(TRI_BASE, c)
    row = lax.broadcasted_iota(jnp.int32, (c, c), 0)
    col = lax.broadcasted_iota(jnp.int32, (c, c), 1)
    eye = (row == col).astype(F32)

    def neumann(m, order):
        x = eye - m
        p = m
        k = 1
        while 2 * k < order:
            p = _bdot(p, p)
            x = x + _bdot(x, p)
            k *= 2
        return x

    if c == base:
        return neumann(a, base)
    shift = base.bit_length() - 1
    same = (row >> shift) == (col >> shift)
    dinv = neumann(jnp.where(same, a, 0.0), base)
    e = _bdot(dinv, jnp.where(same, 0.0, a))
    return _bdot(neumann(e, c // base), dinv)


def _gdn_body(qkv_ref, z_ref, gt_ref, buf_ref, cw_ref, ad_ref, gn_ref, s0_ref,
              o_ref, s_ref, xp_ref, *, tb):
    t = pl.program_id(1)

    @pl.when(t == 0)
    def _():
        s_ref[...] = s0_ref[...]
        xp_ref[0:8, :] = buf_ref[0]

    xp_ref[8:8 + tb, :] = qkv_ref[...]
    y = cw_ref[0:1, :] * xp_ref[5:5 + tb, :]
    y = y + cw_ref[1:2, :] * xp_ref[6:6 + tb, :]
    y = y + cw_ref[2:3, :] * xp_ref[7:7 + tb, :]
    y = _silu(y + cw_ref[3:4, :] * xp_ref[8:8 + tb, :])
    xp_ref[0:8, :] = xp_ref[tb:tb + 8, :]

    def heads(base):
        return jnp.stack([y[:, base + h * DH:base + (h + 1) * DH] for h in range(H_A)])

    def l2(x):
        return x * lax.rsqrt(jnp.sum(x * x, axis=-1, keepdims=True) + EPS)

    q = l2(heads(0)) * (DH ** -0.5)
    k = l2(heads(D_A))
    v = heads(2 * D_A)

    gt = gt_ref[...]
    beta = jax.nn.sigmoid(gt)
    g = -jnp.exp(ad_ref[0:1, :]) * _softplus(gt + ad_ref[1:2, :])
    row = lax.broadcasted_iota(jnp.int32, (tb, tb), 0)
    col = lax.broadcasted_iota(jnp.int32, (tb, tb), 1)
    tri_incl = col <= row
    tri_strict = col < row
    gc = _dot3(tri_incl.astype(F32), g)
    pad = max(LANES - tb, 0)
    gc_rows = jnp.concatenate([gc, jnp.zeros((pad, LANES), F32)], axis=0) if pad else gc
    gct = gc_rows.T

    bcol = jnp.stack([beta[:, h:h + 1] for h in range(H_A)])
    gcol = jnp.stack([gc[:, GATE_LANE + h:GATE_LANE + h + 1] for h in range(H_A)])
    grow = jnp.stack([gct[GATE_LANE + h:GATE_LANE + h + 1, 0:tb] for h in range(H_A)])
    decay = jnp.where(tri_incl, jnp.exp(jnp.where(tri_incl, gcol - grow, 0.0)), 0.0)
    egc = jnp.exp(gcol)
    kb = k * bcol

    amat = jnp.where(tri_strict, _bdot(kb, k, _B_NT) * decay, 0.0)
    tinv = _tri_inv(amat)
    uw = _bdot(tinv, jnp.concatenate([v * bcol, kb * egc], axis=-1))
    u, w = uw[:, :, :DH], uw[:, :, DH:]
    qk = jnp.where(tri_incl, _bdot(q, k, _B_NT) * decay, 0.0)

    s = s_ref[0]
    wq_s = _bdot(jnp.concatenate([w, q * egc], axis=1), s)
    v_new = u - wq_s[:, :tb]
    o = wq_s[:, tb:] + _bdot(qk, v_new)
    g_last = gcol[:, tb - 1:tb, :]
    k_dec = k * jnp.exp(g_last - gcol)
    s_ref[0] = s * jnp.exp(g_last) + _bdot(k_dec, v_new, _B_TN)

    on = _rms(o, gn_ref[...])
    for h in range(H_A):
        gate = _silu(z_ref[:, h * DH:(h + 1) * DH])
        o_ref[:, h * DH:(h + 1) * DH] = (on[h] * gate).astype(BF16)


def _gdn(ua, gates, buf8, conv_w, a_log, dt_bias, out_norm, s0, *, batch, seq, tb):
    nt = seq // tb
    ad = jnp.zeros((8, LANES), F32)
    ad = ad.at[0, GATE_LANE:GATE_LANE + H_A].set(a_log).at[1, GATE_LANE:GATE_LANE + H_A].set(dt_bias)
    cw = jnp.zeros((8, 3 * D_A), F32).at[:CONV_W].set(conv_w)
    return pl.pallas_call(
        functools.partial(_gdn_body, tb=tb),
        grid=(batch, nt),
        in_specs=[
            pl.BlockSpec((tb, 3 * D_A), lambda b, t: (b * nt + t, 0)),
            pl.BlockSpec((tb, D_A), lambda b, t: (b * nt + t, 3)),
            pl.BlockSpec((tb, LANES), lambda b, t: (b * nt + t, 0)),
            pl.BlockSpec((1, 8, 3 * D_A), lambda b, t: (b, 0, 0)),
            pl.BlockSpec((8, 3 * D_A), lambda b, t: (0, 0)),
            pl.BlockSpec((8, LANES), lambda b, t: (0, 0)),
            pl.BlockSpec((1, DH), lambda b, t: (0, 0)),
            pl.BlockSpec((1, H_A, DH, DH), lambda b, t: (b, 0, 0, 0)),
        ],
        out_specs=[
            pl.BlockSpec((tb, D_A), lambda b, t: (b * nt + t, 0)),
            pl.BlockSpec((1, H_A, DH, DH), lambda b, t: (b, 0, 0, 0)),
        ],
        out_shape=[
            jax.ShapeDtypeStruct((batch * seq, D_A), BF16),
            jax.ShapeDtypeStruct((batch, H_A, DH, DH), F32),
        ],
        scratch_shapes=[pltpu.VMEM((tb + 8, 3 * D_A), F32)],
        compiler_params=_params(("parallel", "arbitrary"), 40),
        name="gdn",
    )(ua, ua, gates, buf8, cw, ad, out_norm.reshape(1, DH), s0)


def _lambda(lam_ref, lam_init):
    l1 = jnp.sum(lam_ref[0:1, :] * lam_ref[1:2, :], axis=-1, keepdims=True)
    l2 = jnp.sum(lam_ref[2:3, :] * lam_ref[3:4, :], axis=-1, keepdims=True)
    return jnp.exp(l1) - jnp.exp(l2) + lam_init


def _attn_prompt_body(lam_ref, sg_ref, q_ref, k_ref, vt_ref, o_ref, acc_ref, *, tq, lam_init):
    qi = pl.program_id(1)
    hw = 2 * DH
    acc_ref[...] = jnp.zeros(acc_ref.shape, F32)
    ones = jnp.ones((acc_ref.shape[1] - hw, tq), BF16)

    def scores(j, sub, masked):
        cols = slice(sub * DH, (sub + 1) * DH)
        ks = k_ref[pl.ds(pl.multiple_of(j * tq, tq), tq), cols]
        st = lax.dot_general(ks, q_ref[:, cols], _NT, preferred_element_type=F32)
        if masked:
            key = lax.broadcasted_iota(jnp.int32, (tq, tq), 0)
            qry = lax.broadcasted_iota(jnp.int32, (tq, tq), 1)
            st = jnp.where(key <= qry, st, -jnp.inf)
        return st

    def accumulate(j, sub, st, m_prev):
        vt = jnp.concatenate([vt_ref[j], ones], axis=0)
        m_new = jnp.maximum(m_prev, jnp.max(st, axis=0, keepdims=True))
        corr = jnp.exp2(m_prev - m_new)
        p = jnp.exp2(st - m_new).astype(BF16)
        acc_ref[sub] = acc_ref[sub] * corr + jnp.dot(vt, p, preferred_element_type=F32)
        return m_new

    def block(j, maxes, masked):
        st0 = scores(j, 0, masked)
        st1 = scores(j, 1, masked)
        m0 = accumulate(j, 0, st0, maxes[0])
        return m0, accumulate(j, 1, st1, maxes[1])

    def pair(j, maxes, mask_second):
        st0 = scores(j, 0, False)
        st1 = scores(j, 1, False)
        m0 = accumulate(j, 0, st0, maxes[0])
        st2 = scores(j + 1, 0, mask_second)
        m1 = accumulate(j, 1, st1, maxes[1])
        st3 = scores(j + 1, 1, mask_second)
        m0 = accumulate(j + 1, 0, st2, m0)
        return m0, accumulate(j + 1, 1, st3, m1)

    init = jnp.full((1, tq), -jnp.inf, F32)
    n_pairs = lax.shift_right_logical(qi, 1)
    maxes = lax.fori_loop(0, n_pairs, lambda p, mx: pair(2 * p, mx, False), (init, init))

    @pl.when((qi & 1) == 1)
    def _():
        pair(qi - 1, maxes, True)

    @pl.when((qi & 1) == 0)
    def _():
        block(qi, maxes, True)

    lam = _lambda(lam_ref, lam_init)
    a0, a1 = acc_ref[0], acc_ref[1]
    ot = a0[:hw] / a0[hw:hw + 1] - lam * (a1[:hw] / a1[hw:hw + 1])
    o_ref[...] = (_rms(ot.T, sg_ref[...]) * (1.0 - lam_init)).astype(BF16)


def _attn_prompt(q, k, vt, lam_rows, subln, lam_init, *, tq):
    s_len = q.shape[0]
    hw = 2 * DH
    return pl.pallas_call(
        functools.partial(_attn_prompt_body, tq=tq, lam_init=lam_init),
        grid=(H_B, s_len // tq),
        in_specs=[
            pl.BlockSpec((8, DH), lambda h, i: (0, 0)),
            pl.BlockSpec((1, hw), lambda h, i: (0, 0)),
            pl.BlockSpec((tq, hw), lambda h, i: (i, h)),
            pl.BlockSpec((s_len, hw), lambda h, i: (0, h)),
            pl.BlockSpec((s_len // tq, hw, tq), lambda h, i: (0, h, 0)),
        ],
        out_specs=pl.BlockSpec((tq, hw), lambda h, i: (i, h)),
        out_shape=jax.ShapeDtypeStruct((s_len, H_B * hw), BF16),
        scratch_shapes=[pltpu.VMEM((2, hw + BF16_ROWS, tq), F32)],
        compiler_params=_params(("parallel", "arbitrary"), 52),
        name="attn_prompt",
    )(lam_rows, subln.reshape(1, hw), q, k, vt)


def _attn_sample_body(pt_ref, lam_ref, sg_ref, q_ref, kn_ref, vn_ref, *rest, group, n_steps, n_new, lam_init):
    kc_refs = rest[:group]
    vc_refs = rest[group:2 * group]
    o_ref, m_ref, l_ref, acc_ref = rest[2 * group:]
    step = pl.program_id(1)
    rows = q_ref.shape[1]
    q = q_ref[0]

    def update(s, vals):
        m_prev = m_ref[:, 0:1]
        m_new = jnp.maximum(m_prev, jnp.max(s, axis=-1, keepdims=True))
        corr = jnp.exp2(m_prev - m_new)
        p = jnp.exp2(s - m_new)
        l_new = corr * l_ref[:, 0:1] + jnp.sum(p, axis=-1, keepdims=True)
        pb = p.astype(BF16)
        pv = None
        width = s.shape[1] // len(vals)
        for i, val in enumerate(vals):
            part = jnp.dot(pb[:, i * width:(i + 1) * width], val, preferred_element_type=F32)
            pv = part if pv is None else pv + part
        acc_ref[...] = acc_ref[...] * corr + pv
        m_ref[...] = jnp.broadcast_to(m_new, m_ref.shape)
        l_ref[...] = jnp.broadcast_to(l_new, l_ref.shape)

    @pl.when(step == 0)
    def _():
        m_ref[...] = jnp.full(m_ref.shape, -jnp.inf, F32)
        l_ref[...] = jnp.zeros(l_ref.shape, F32)
        acc_ref[...] = jnp.zeros(acc_ref.shape, F32)
        kn = kn_ref[0]
        s = lax.dot_general(q, kn, _NT, preferred_element_type=F32)
        tok = lax.broadcasted_iota(jnp.int32, s.shape, 0) & (n_new - 1)
        key = lax.broadcasted_iota(jnp.int32, s.shape, 1)
        s = jnp.where(key <= tok, s, -jnp.inf)
        update(s, [vn_ref[0]])

    def page(ref):
        parts = [ref[pl.ds(half * H_B + h, PAGE_SIZE, stride=2 * H_B), :] for h in range(H_B) for half in range(2)]
        return jnp.concatenate(parts, axis=1).astype(BF16)

    s_parts = [lax.dot_general(q, page(kc), _NT, preferred_element_type=F32) for kc in kc_refs]
    update(jnp.concatenate(s_parts, axis=-1), [page(vc) for vc in vc_refs])

    @pl.when(step == n_steps - 1)
    def _():
        lam = _lambda(lam_ref, lam_init)
        hw = 2 * DH
        for h in range(H_B):
            r0 = h * 2 * n_new
            a0 = acc_ref[r0:r0 + n_new, h * hw:(h + 1) * hw] / l_ref[r0:r0 + n_new, 0:1]
            a1 = acc_ref[r0 + n_new:r0 + 2 * n_new, h * hw:(h + 1) * hw] / l_ref[r0 + n_new:r0 + 2 * n_new, 0:1]
            o = a0 - lam * a1
            o_ref[0, :, h * hw:(h + 1) * hw] = (_rms(o, sg_ref[...]) * (1.0 - lam_init)).astype(BF16)


def _attn_sample(qbd, k_new, v_new, cache_k, cache_v, page_ids, lam_rows, subln, lam_init, *, n_pages, group):
    batch, rows, width = qbd.shape
    n_new = rows // (2 * H_B)
    n_steps = n_pages // group
    hw = 2 * DH
    page_rows = PAGE_SIZE * 2 * H_B

    def cache_spec(g):
        return pl.BlockSpec((page_rows, DH), lambda b, s, pt: (pt[b * n_pages + s * group + g], 0))

    in_specs = [
        pl.BlockSpec((8, DH), lambda b, s, pt: (0, 0)),
        pl.BlockSpec((1, hw), lambda b, s, pt: (0, 0)),
        pl.BlockSpec((1, rows, width), lambda b, s, pt: (b, 0, 0)),
        pl.BlockSpec((1, k_new.shape[1], width), lambda b, s, pt: (b, 0, 0)),
        pl.BlockSpec((1, v_new.shape[1], width), lambda b, s, pt: (b, 0, 0)),
    ] + [cache_spec(g) for g in range(group)] + [cache_spec(g) for g in range(group)]
    grid_spec = pltpu.PrefetchScalarGridSpec(
        num_scalar_prefetch=1,
        grid=(batch, n_steps),
        in_specs=in_specs,
        out_specs=pl.BlockSpec((1, n_new, width), lambda b, s, pt: (b, 0, 0)),
        scratch_shapes=[
            pltpu.VMEM((rows, LANES), F32),
            pltpu.VMEM((rows, LANES), F32),
            pltpu.VMEM((rows, width), F32),
        ],
    )
    return pl.pallas_call(
        functools.partial(_attn_sample_body, group=group, n_steps=n_steps, n_new=n_new, lam_init=lam_init),
        grid_spec=grid_spec,
        out_shape=jax.ShapeDtypeStruct((batch, n_new, width), BF16),
        compiler_params=_params(("parallel", "arbitrary"), 40),
        name="attn_sample",
    )(page_ids, lam_rows, subln.reshape(1, hw), qbd, k_new, v_new,
      *([cache_k] * group), *([cache_v] * group))


def _outproj_body(h_ref, a_ref, b_ref, wa_ref, wb_ref, o_ref):
    mix = jnp.dot(a_ref[...], wa_ref[...], preferred_element_type=F32)
    mix = mix + jnp.dot(b_ref[...], wb_ref[...], preferred_element_type=F32)
    o_ref[...] = h_ref[...] + mix


def _outproj(h, oa, ob, wa, wb, *, tm, tn=512):
    t, d = h.shape
    return pl.pallas_call(
        _outproj_body,
        grid=(t // tm, d // tn),
        in_specs=[
            pl.BlockSpec((tm, tn), lambda i, j: (i, j)),
            pl.BlockSpec((tm, D_A), lambda i, j: (i, 0)),
            pl.BlockSpec((tm, D_B), lambda i, j: (i, 0)),
            pl.BlockSpec((D_A, tn), lambda i, j: (0, j)),
            pl.BlockSpec((D_B, tn), lambda i, j: (0, j)),
        ],
        out_specs=pl.BlockSpec((tm, tn), lambda i, j: (i, j)),
        out_shape=jax.ShapeDtypeStruct((t, d), F32),
        compiler_params=_params(("parallel", "arbitrary"), 40),
        name="outproj",
    )(h, oa, ob, wa, wb)


def _layer(x, conv_buf, s_delta, attend, w, *, batch, seq, tm, gdn_tb, v_block=None):
    h, hn = _ffn(x, w["ffn1_norm"], w["ffn1_wg"], w["ffn1_wu"], w["ffn1_wd"], w["mix_norm"], tm=tm)
    ptm = min(2 * tm, x.shape[0])
    (ua,) = _proj(hn, w["w_qkvz"], "plain", tm=ptm, tn=512)
    (gates,) = _proj(hn, w["w_gate"], "plain", tm=ptm, tn=LANES)
    (qb,) = _proj(hn, w["w_qb"], "qnorm", w["q_norm"], tm=ptm, tn=512)
    kb32, kb16 = _proj(hn, w["w_kb"], "knorm", w["k_norm"], tm=ptm, tn=512)
    if v_block is None:
        vb32, vb16 = _proj(hn, w["w_vb"], "dual", tm=ptm, tn=512)
    else:
        vb32, vb16 = _proj(hn, w["w_vb"], "dual_t", tm=ptm, tn=512, tk=v_block)

    buf8 = jnp.pad(conv_buf, ((0, 0), (8 - (CONV_W - 1), 0), (0, 0)))
    oa, new_s = _gdn(ua, gates, buf8, w["conv_w"], w["A_log"], w["dt_bias"], w["gdn_out_norm"], s_delta,
                     batch=batch, seq=seq, tb=gdn_tb)
    assert seq >= CONV_W - 1
    new_buf = ua.reshape(batch, seq, 4 * D_A)[:, seq - (CONV_W - 1):, :3 * D_A]

    ob = attend(qb, kb16, vb16)
    h2 = _outproj(h, oa, ob, w["w_out_a"], w["w_out_b"], tm=ptm)
    y = _ffn(h2, w["ffn2_norm"], w["ffn2_wg"], w["ffn2_wu"], w["ffn2_wd"], tm=tm)
    return y, kb32, vb32, new_s, new_buf


def kernel(x_prompt, x_sample, cache_k, cache_v, page_table, state_delta, state_conv, ffn1_norm, ffn1_w_gate, ffn1_w_up, ffn1_w_down, mix_norm, w_in, conv_w, A_log, dt_bias, gdn_out_norm, q_norm, k_norm, lambda_q1, lambda_k1, lambda_q2, lambda_k2, diff_subln, w_out, ffn2_norm, ffn2_w_gate, ffn2_w_up, ffn2_w_down):
    depth = ffn1_norm.shape[0]
    bp, sp, _ = x_prompt.shape
    assert bp == 1
    bs, ss, _ = x_sample.shape
    n_pages = page_table.shape[1]
    n_pool = cache_k.shape[1]
    hw = 2 * DH
    width = H_B * hw

    yp = x_prompt.reshape(bp * sp, D_MODEL)
    ys = x_sample.reshape(bs * ss, D_MODEL)
    def page_rows(cache):
        x = cache.reshape(depth * n_pool, PAGE_SIZE, H_B, 2, DH)
        return jnp.transpose(x, (0, 1, 3, 2, 4)).reshape(depth * n_pool * PAGE_SIZE * 2 * H_B, DH)

    cache_k2 = page_rows(cache_k)
    cache_v2 = page_rows(cache_v)
    outs = [[] for _ in range(8)]
    for l in range(depth):
        lam_init = 0.8 - 0.6 * math.exp(-0.3 * l)
        wl = w_in[l]
        small = jnp.zeros((D_MODEL, LANES), F32).at[:, :2 * H_A].set(wl[:, 4 * D_A:4 * D_A + 2 * H_A])
        off = 4 * D_A + 2 * H_A
        w = {
            "ffn1_norm": ffn1_norm[l], "mix_norm": mix_norm[l], "ffn2_norm": ffn2_norm[l],
            "ffn1_wg": ffn1_w_gate[l].astype(BF16), "ffn1_wu": ffn1_w_up[l].astype(BF16),
            "ffn1_wd": ffn1_w_down[l].astype(BF16),
            "ffn2_wg": ffn2_w_gate[l].astype(BF16), "ffn2_wu": ffn2_w_up[l].astype(BF16),
            "ffn2_wd": ffn2_w_down[l].astype(BF16),
            "w_qkvz": wl[:, :4 * D_A].astype(BF16), "w_gate": small.astype(BF16),
            "w_qb": wl[:, off:off + D_B].astype(BF16), "w_kb": wl[:, off + D_B:off + 2 * D_B].astype(BF16),
            "w_vb": wl[:, off + 2 * D_B:off + 3 * D_B].astype(BF16),
            "w_out_a": w_out[l, :D_A].astype(BF16), "w_out_b": w_out[l, D_A:].astype(BF16),
            "conv_w": conv_w[l], "A_log": A_log[l], "dt_bias": dt_bias[l], "gdn_out_norm": gdn_out_norm[l],
            "q_norm": q_norm[l], "k_norm": k_norm[l],
        }
        lam_rows = jnp.zeros((8, DH), F32)
        lam_rows = lam_rows.at[0].set(lambda_q1[l]).at[1].set(lambda_k1[l]).at[2].set(lambda_q2[l]).at[3].set(lambda_k2[l])
        subln = diff_subln[l]

        tq = 512

        def attend_prompt(q, k, vt):
            return _attn_prompt(q, k, vt, lam_rows, subln, lam_init, tq=tq)

        def attend_sample(q, k, v):
            q4 = q.reshape(bs, ss, 2 * H_B, DH)
            eye = jnp.eye(2 * H_B, dtype=BF16)
            qbd = (q4[:, :, :, None, :] * eye[None, None, :, :, None])
            qbd = jnp.transpose(qbd, (0, 2, 1, 3, 4)).reshape(bs, 2 * H_B * ss, width)
            k_new = jnp.pad(k.reshape(bs, ss, width), ((0, 0), (0, 16 - ss), (0, 0)))
            v_new = jnp.pad(v.reshape(bs, ss, width), ((0, 0), (0, 16 - ss), (0, 0)))
            page_ids = (page_table + l * n_pool).reshape(-1).astype(jnp.int32)
            o = _attn_sample(qbd, k_new, v_new, cache_k2, cache_v2, page_ids, lam_rows, subln, lam_init,
                             n_pages=n_pages, group=8)
            return o.reshape(bs * ss, width)

        buf0 = jnp.zeros((bp, CONV_W - 1, 3 * D_A), F32)
        s0 = jnp.zeros((bp, H_A, DH, DH), F32)
        yp, kp, vp, s_p, b_p = _layer(yp, buf0, s0, attend_prompt, w, batch=bp, seq=sp, tm=512,
                                      gdn_tb=128, v_block=tq)
        ys, k_s, v_s, s_s, b_s = _layer(ys, state_conv[l], state_delta[l], attend_sample, w, batch=bs, seq=ss,
                                        tm=bs * ss, gdn_tb=ss)
        for lst, val in zip(outs, (kp.reshape(bp, sp, H_B, hw), vp.reshape(bp, sp, H_B, hw),
                                   k_s.reshape(bs, ss, H_B, hw), v_s.reshape(bs, ss, H_B, hw),
                                   s_p, b_p, s_s, b_s)):
            lst.append(val)
    return (yp.reshape(bp, sp, D_MODEL), ys.reshape(bs, ss, D_MODEL), *[jnp.stack(o) for o in outs])
```

```python
import functools
import math

import jax
import jax.numpy as jnp
from jax import lax
from jax.experimental import pallas as pl
from jax.experimental.pallas import tpu as pltpu

F32 = jnp.float32
BF16 = jnp.bfloat16

D_MODEL = 2048
D_A = 1024
D_B = 1024
DH = 128
H_A = D_A // DH
H_B = D_B // (2 * DH)
CONV_W = 4
D_FF = 5632
EPS = 1e-6
PAGE_SIZE = 128
LANES = 128
BF16_ROWS = 16
LOG2E = math.log2(math.e)
GATE_LANE = 8
TRI_BASE = 16

MIB = 1024 * 1024


def _params(semantics, vmem_mib):
    return pltpu.CompilerParams(dimension_semantics=semantics, vmem_limit_bytes=vmem_mib * MIB)


def _rms(x, g):
    return x * lax.rsqrt(jnp.mean(x * x, axis=-1, keepdims=True) + EPS) * g


def _silu(x):
    return x * jax.nn.sigmoid(x)


def _softplus(x):
    return jnp.maximum(x, 0.0) + jnp.log1p(jnp.exp(-jnp.abs(x)))


def _ffn_body(x_ref, g_ref, wg_ref, wu_ref, wd_ref, *rest, nf, with_norm):
    if with_norm:
        ng_ref, o_ref, on_ref, xn_ref = rest
    else:
        o_ref, xn_ref = rest
    f = pl.program_id(1)

    @pl.when(f == 0)
    def _():
        xn_ref[...] = _rms(x_ref[...], g_ref[...]).astype(BF16)
        o_ref[...] = jnp.zeros_like(o_ref)

    xn = xn_ref[...]
    a = jnp.dot(xn, wg_ref[...], preferred_element_type=F32)
    b = jnp.dot(xn, wu_ref[...], preferred_element_type=F32)
    hid = (_silu(a) * b).astype(BF16)
    o_ref[...] += jnp.dot(hid, wd_ref[...], preferred_element_type=F32)

    @pl.when(f == nf - 1)
    def _():
        y = x_ref[...] + 0.5 * o_ref[...]
        o_ref[...] = y
        if with_norm:
            on_ref[...] = _rms(y, ng_ref[...]).astype(BF16)


def _ffn(x, g, wg, wu, wd, next_g=None, *, tm, tf=512):
    t, d = x.shape
    nf = D_FF // tf
    with_norm = next_g is not None
    in_specs = [
        pl.BlockSpec((tm, d), lambda i, f: (i, 0)),
        pl.BlockSpec((1, d), lambda i, f: (0, 0)),
        pl.BlockSpec((d, tf), lambda i, f: (0, f)),
        pl.BlockSpec((d, tf), lambda i, f: (0, f)),
        pl.BlockSpec((tf, d), lambda i, f: (f, 0)),
    ]
    args = [x, g.reshape(1, d), wg, wu, wd]
    out_shape = [jax.ShapeDtypeStruct((t, d), F32)]
    out_specs = [pl.BlockSpec((tm, d), lambda i, f: (i, 0))]
    if with_norm:
        in_specs.append(pl.BlockSpec((1, d), lambda i, f: (0, 0)))
        args.append(next_g.reshape(1, d))
        out_shape.append(jax.ShapeDtypeStruct((t, d), BF16))
        out_specs.append(pl.BlockSpec((tm, d), lambda i, f: (i, 0)))
    res = pl.pallas_call(
        functools.partial(_ffn_body, nf=nf, with_norm=with_norm),
        grid=(t // tm, nf),
        in_specs=in_specs,
        out_specs=out_specs,
        out_shape=out_shape,
        scratch_shapes=[pltpu.VMEM((tm, d), BF16)],
        compiler_params=_params(("parallel", "arbitrary"), 48),
        name="ffn",
    )(*args)
    return res if with_norm else res[0]


def _store_cache_rows(o_ref, y, c):
    head, half = divmod(c, 2)
    o_ref[pl.ds(half * H_B + head, y.shape[0], stride=2 * H_B), :] = y


def _proj_body(x_ref, w_ref, *rest, mode):
    acc = jnp.dot(x_ref[...], w_ref[...], preferred_element_type=F32)
    groups = [(c, slice(c * DH, (c + 1) * DH)) for c in range(acc.shape[1] // DH)]
    if mode == "plain":
        (o_ref,) = rest
        o_ref[...] = acc
    elif mode == "qnorm":
        g_ref, o_ref = rest
        for c, cols in groups:
            o_ref[:, cols] = (_rms(acc[:, cols], g_ref[...]) * (LOG2E * DH ** -0.5)).astype(BF16)
    elif mode == "knorm":
        g_ref, o32_ref, o16_ref = rest
        for c, cols in groups:
            y = _rms(acc[:, cols], g_ref[...])
            _store_cache_rows(o32_ref, y, c)
            o16_ref[:, cols] = y.astype(BF16)
    else:
        o32_ref, o16_ref = rest
        for c, cols in groups:
            _store_cache_rows(o32_ref, acc[:, cols], c)
        if mode == "dual":
            o16_ref[...] = acc.astype(BF16)
        else:
            tk = o16_ref.shape[2]
            for r in range(o16_ref.shape[0]):
                o16_ref[r] = acc[r * tk:(r + 1) * tk, :].T.astype(BF16)


def _proj(xn, w, mode, gain=None, *, tm, tn, tk=None):
    t, d = xn.shape
    n = w.shape[1]
    in_specs = [pl.BlockSpec((tm, d), lambda i, j: (i, 0)), pl.BlockSpec((d, tn), lambda i, j: (0, j))]
    args = [xn, w]
    if mode in ("qnorm", "knorm"):
        in_specs.append(pl.BlockSpec((1, DH), lambda i, j: (0, 0)))
        args.append(gain.reshape(1, DH))
    o_spec = pl.BlockSpec((tm, tn), lambda i, j: (i, j))
    if mode == "plain":
        out_shape, out_specs = [jax.ShapeDtypeStruct((t, n), F32)], [o_spec]
    elif mode == "qnorm":
        out_shape, out_specs = [jax.ShapeDtypeStruct((t, n), BF16)], [o_spec]
    else:
        assert tn == n == H_B * 2 * DH
        rows = n // DH
        out_shape = [jax.ShapeDtypeStruct((t * rows, DH), F32)]
        out_specs = [pl.BlockSpec((tm * rows, DH), lambda i, j: (i, 0))]
        if mode == "dual_t":
            out_shape.append(jax.ShapeDtypeStruct((t // tk, n, tk), BF16))
            out_specs.append(pl.BlockSpec((tm // tk, tn, tk), lambda i, j: (i, j, 0)))
        else:
            out_shape.append(jax.ShapeDtypeStruct((t, n), BF16))
            out_specs.append(o_spec)
    return pl.pallas_call(
        functools.partial(_proj_body, mode=mode),
        grid=(t // tm, n // tn),
        in_specs=in_specs,
        out_specs=out_specs,
        out_shape=out_shape,
        compiler_params=_params(("parallel", "arbitrary"), 40),
        name="proj_" + mode,
    )(*args)


def _split(x):
    hi = x.astype(BF16)
    lo = (x - hi.astype(F32)).astype(BF16)
    return hi, lo


def _dot3(a, b, dims=(((1,), (0,)), ((), ()))):
    ah, al = _split(a)
    bh, bl = _split(b)
    dg = functools.partial(lax.dot_general, dimension_numbers=dims, preferred_element_type=F32)
    return dg(ah, bh) + (dg(ah, bl) + dg(al, bh))


_NT = (((1,), (1,)), ((), ()))
_TN = (((0,), (0,)), ((), ()))


_B_NN = (((2,), (1,)), ((0,), (0,)))
_B_NT = (((2,), (2,)), ((0,), (0,)))
_B_TN = (((1,), (1,)), ((0,), (0,)))


def _bdot(a, b, dims=_B_NN):
    return lax.dot_general(a.astype(BF16), b.astype(BF16), dims, preferred_element_type=F32)


def _tri_inv(a):
    c = a.shape[-1]
    base = min(TRI_BASE, c)
    row = lax.broadcasted_iota(jnp.int32, (c, c), 0)
    col = lax.broadcasted_iota(jnp.int32, (c, c), 1)
    eye = (row == col).astype(F32)

    def neumann(m, order):
        x = eye - m
        p = m
        k = 1
        while 2 * k < order:
            p = _bdot(p, p)
            x = x + _bdot(x, p)
            k *= 2
        return x

    if c == base:
        return neumann(a, base)
    shift = base.bit_length() - 1
    same = (row >> shift) == (col >> shift)
    dinv = neumann(jnp.where(same, a, 0.0), base)
    e = _bdot(dinv, jnp.where(same, 0.0, a))
    return _bdot(neumann(e, c // base), dinv)


def _gdn_body(qkv_ref, z_ref, gt_ref, buf_ref, cw_ref, ad_ref, gn_ref, s0_ref,
              o_ref, s_ref, xp_ref, *, tb):
    t = pl.program_id(1)

    @pl.when(t == 0)
    def _():
        s_ref[...] = s0_ref[...]
        xp_ref[0:8, :] = buf_ref[0]

    xp_ref[8:8 + tb, :] = qkv_ref[...]
    y = cw_ref[0:1, :] * xp_ref[5:5 + tb, :]
    y = y + cw_ref[1:2, :] * xp_ref[6:6 + tb, :]
    y = y + cw_ref[2:3, :] * xp_ref[7:7 + tb, :]
    y = _silu(y + cw_ref[3:4, :] * xp_ref[8:8 + tb, :])
    xp_ref[0:8, :] = xp_ref[tb:tb + 8, :]

    def heads(base):
        return jnp.stack([y[:, base + h * DH:base + (h + 1) * DH] for h in range(H_A)])

    def l2(x):
        return x * lax.rsqrt(jnp.sum(x * x, axis=-1, keepdims=True) + EPS)

    q = l2(heads(0)) * (DH ** -0.5)
    k = l2(heads(D_A))
    v = heads(2 * D_A)

    gt = gt_ref[...]
    beta = jax.nn.sigmoid(gt)
    g = -jnp.exp(ad_ref[0:1, :]) * _softplus(gt + ad_ref[1:2, :])
    row = lax.broadcasted_iota(jnp.int32, (tb, tb), 0)
    col = lax.broadcasted_iota(jnp.int32, (tb, tb), 1)
    tri_incl = col <= row
    tri_strict = col < row
    gc = _dot3(tri_incl.astype(F32), g)
    pad = max(LANES - tb, 0)
    gc_rows = jnp.concatenate([gc, jnp.zeros((pad, LANES), F32)], axis=0) if pad else gc
    gct = gc_rows.T

    bcol = jnp.stack([beta[:, h:h + 1] for h in range(H_A)])
    gcol = jnp.stack([gc[:, GATE_LANE + h:GATE_LANE + h + 1] for h in range(H_A)])
    grow = jnp.stack([gct[GATE_LANE + h:GATE_LANE + h + 1, 0:tb] for h in range(H_A)])
    decay = jnp.where(tri_incl, jnp.exp(jnp.where(tri_incl, gcol - grow, 0.0)), 0.0)
    egc = jnp.exp(gcol)
    kb = k * bcol

    amat = jnp.where(tri_strict, _bdot(kb, k, _B_NT) * decay, 0.0)
    tinv = _tri_inv(amat)
    uw = _bdot(tinv, jnp.concatenate([v * bcol, kb * egc], axis=-1))
    u, w = uw[:, :, :DH], uw[:, :, DH:]
    qk = jnp.where(tri_incl, _bdot(q, k, _B_NT) * decay, 0.0)

    s = s_ref[0]
    wq_s = _bdot(jnp.concatenate([w, q * egc], axis=1), s)
    v_new = u - wq_s[:, :tb]
    o = wq_s[:, tb:] + _bdot(qk, v_new)
    g_last = gcol[:, tb - 1:tb, :]
    k_dec = k * jnp.exp(g_last - gcol)
    s_ref[0] = s * jnp.exp(g_last) + _bdot(k_dec, v_new, _B_TN)

    on = _rms(o, gn_ref[...])
    for h in range(H_A):
        gate = _silu(z_ref[:, h * DH:(h + 1) * DH])
        o_ref[:, h * DH:(h + 1) * DH] = (on[h] * gate).astype(BF16)


def _gdn(ua, gates, buf8, conv_w, a_log, dt_bias, out_norm, s0, *, batch, seq, tb):
    nt = seq // tb
    ad = jnp.zeros((8, LANES), F32)
    ad = ad.at[0, GATE_LANE:GATE_LANE + H_A].set(a_log).at[1, GATE_LANE:GATE_LANE + H_A].set(dt_bias)
    cw = jnp.zeros((8, 3 * D_A), F32).at[:CONV_W].set(conv_w)
    return pl.pallas_call(
        functools.partial(_gdn_body, tb=tb),
        grid=(batch, nt),
        in_specs=[
            pl.BlockSpec((tb, 3 * D_A), lambda b, t: (b * nt + t, 0)),
            pl.BlockSpec((tb, D_A), lambda b, t: (b * nt + t, 3)),
            pl.BlockSpec((tb, LANES), lambda b, t: (b * nt + t, 0)),
            pl.BlockSpec((1, 8, 3 * D_A), lambda b, t: (b, 0, 0)),
            pl.BlockSpec((8, 3 * D_A), lambda b, t: (0, 0)),
            pl.BlockSpec((8, LANES), lambda b, t: (0, 0)),
            pl.BlockSpec((1, DH), lambda b, t: (0, 0)),
            pl.BlockSpec((1, H_A, DH, DH), lambda b, t: (b, 0, 0, 0)),
        ],
        out_specs=[
            pl.BlockSpec((tb, D_A), lambda b, t: (b * nt + t, 0)),
            pl.BlockSpec((1, H_A, DH, DH), lambda b, t: (b, 0, 0, 0)),
        ],
        out_shape=[
            jax.ShapeDtypeStruct((batch * seq, D_A), BF16),
            jax.ShapeDtypeStruct((batch, H_A, DH, DH), F32),
        ],
        scratch_shapes=[pltpu.VMEM((tb + 8, 3 * D_A), F32)],
        compiler_params=_params(("parallel", "arbitrary"), 40),
        name="gdn",
    )(ua, ua, gates, buf8, cw, ad, out_norm.reshape(1, DH), s0)


def _lambda(lam_ref, lam_init):
    l1 = jnp.sum(lam_ref[0:1, :] * lam_ref[1:2, :], axis=-1, keepdims=True)
    l2 = jnp.sum(lam_ref[2:3, :] * lam_ref[3:4, :], axis=-1, keepdims=True)
    return jnp.exp(l1) - jnp.exp(l2) + lam_init


def _attn_prompt_body(lam_ref, sg_ref, q_ref, k_ref, vt_ref, o_ref, acc_ref, s_ref, *, tq, lam_init):
    qi = pl.program_id(1)
    hw = 2 * DH
    acc_ref[...] = jnp.zeros(acc_ref.shape, F32)
    ones = jnp.ones((acc_ref.shape[1] - hw, tq), BF16)

    def scores(j, sub):
        cols = slice(sub * DH, (sub + 1) * DH)
        ks = k_ref[pl.ds(pl.multiple_of(j * tq, tq), tq), cols]
        return lax.dot_general(ks, q_ref[:, cols], _NT, preferred_element_type=F32)

    def accumulate(j, sub, st, m_prev, masked=False):
        if masked:
            key = lax.broadcasted_iota(jnp.int32, (tq, tq), 0)
            qry = lax.broadcasted_iota(jnp.int32, (tq, tq), 1)
            st = jnp.where(key <= qry, st, -jnp.inf)
        vt = jnp.concatenate([vt_ref[j], ones], axis=0)
        m_new = jnp.maximum(m_prev, jnp.max(st, axis=0, keepdims=True))
        corr = jnp.exp2(m_prev - m_new)
        p = jnp.exp2(st - m_new).astype(BF16)
        acc_ref[sub] = acc_ref[sub] * corr + jnp.dot(vt, p, preferred_element_type=F32)
        return m_new

    s_ref[0] = scores(0, 0)
    s_ref[1] = scores(0, 1)

    def pair(p, maxes):
        a, b = 2 * p, 2 * p + 1
        st_b0 = scores(b, 0)
        m0 = accumulate(a, 0, s_ref[0], maxes[0])
        st_b1 = scores(b, 1)
        m1 = accumulate(a, 1, s_ref[1], maxes[1])
        s_ref[0] = scores(b + 1, 0)
        m0 = accumulate(b, 0, st_b0, m0)
        s_ref[1] = scores(b + 1, 1)
        return m0, accumulate(b, 1, st_b1, m1)

    init = jnp.full((1, tq), -jnp.inf, F32)
    n_pairs = lax.shift_right_logical(qi, 1)
    maxes = lax.fori_loop(0, n_pairs, pair, (init, init))

    @pl.when((qi & 1) == 1)
    def _():
        st_d0 = scores(qi, 0)
        m0 = accumulate(qi - 1, 0, s_ref[0], maxes[0])
        st_d1 = scores(qi, 1)
        m1 = accumulate(qi - 1, 1, s_ref[1], maxes[1])
        accumulate(qi, 0, st_d0, m0, masked=True)
        accumulate(qi, 1, st_d1, m1, masked=True)

    @pl.when((qi & 1) == 0)
    def _():
        accumulate(qi, 0, s_ref[0], maxes[0], masked=True)
        accumulate(qi, 1, s_ref[1], maxes[1], masked=True)

    lam = _lambda(lam_ref, lam_init)
    a0, a1 = acc_ref[0], acc_ref[1]
    ot = a0[:hw] / a0[hw:hw + 1] - lam * (a1[:hw] / a1[hw:hw + 1])
    o_ref[...] = (_rms(ot.T, sg_ref[...]) * (1.0 - lam_init)).astype(BF16)


def _attn_prompt(q, k, vt, lam_rows, subln, lam_init, *, tq):
    s_len = q.shape[0]
    hw = 2 * DH
    return pl.pallas_call(
        functools.partial(_attn_prompt_body, tq=tq, lam_init=lam_init),
        grid=(H_B, s_len // tq),
        in_specs=[
            pl.BlockSpec((8, DH), lambda h, i: (0, 0)),
            pl.BlockSpec((1, hw), lambda h, i: (0, 0)),
            pl.BlockSpec((tq, hw), lambda h, i: (i, h)),
            pl.BlockSpec((s_len, hw), lambda h, i: (0, h)),
            pl.BlockSpec((s_len // tq, hw, tq), lambda h, i: (0, h, 0)),
        ],
        out_specs=pl.BlockSpec((tq, hw), lambda h, i: (i, h)),
        out_shape=jax.ShapeDtypeStruct((s_len, H_B * hw), BF16),
        scratch_shapes=[pltpu.VMEM((2, hw + BF16_ROWS, tq), F32), pltpu.VMEM((2, tq, tq), F32)],
        compiler_params=_params(("parallel", "arbitrary"), 52),
        name="attn_prompt",
    )(lam_rows, subln.reshape(1, hw), q, k, vt)


def _attn_sample_body(pt_ref, lam_ref, sg_ref, q_ref, kn_ref, vn_ref, *rest, group, n_steps, n_new, lam_init):
    kc_refs = rest[:group]
    vc_refs = rest[group:2 * group]
    o_ref, m_ref, l_ref, acc_ref = rest[2 * group:]
    step = pl.program_id(1)
    rows = q_ref.shape[1]
    q = q_ref[0]

    def update(s, vals):
        m_prev = m_ref[:, 0:1]
        m_new = jnp.maximum(m_prev, jnp.max(s, axis=-1, keepdims=True))
        corr = jnp.exp2(m_prev - m_new)
        p = jnp.exp2(s - m_new)
        l_new = corr * l_ref[:, 0:1] + jnp.sum(p, axis=-1, keepdims=True)
        pb = p.astype(BF16)
        pv = None
        width = s.shape[1] // len(vals)
        for i, val in enumerate(vals):
            part = jnp.dot(pb[:, i * width:(i + 1) * width], val, preferred_element_type=F32)
            pv = part if pv is None else pv + part
        acc_ref[...] = acc_ref[...] * corr + pv
        m_ref[...] = jnp.broadcast_to(m_new, m_ref.shape)
        l_ref[...] = jnp.broadcast_to(l_new, l_ref.shape)

    @pl.when(step == 0)
    def _():
        m_ref[...] = jnp.full(m_ref.shape, -jnp.inf, F32)
        l_ref[...] = jnp.zeros(l_ref.shape, F32)
        acc_ref[...] = jnp.zeros(acc_ref.shape, F32)
        kn = kn_ref[0]
        s = lax.dot_general(q, kn, _NT, preferred_element_type=F32)
        tok = lax.broadcasted_iota(jnp.int32, s.shape, 0) & (n_new - 1)
        key = lax.broadcasted_iota(jnp.int32, s.shape, 1)
        s = jnp.where(key <= tok, s, -jnp.inf)
        update(s, [vn_ref[0]])

    def page(ref):
        parts = [ref[pl.ds(half * H_B + h, PAGE_SIZE, stride=2 * H_B), :] for h in range(H_B) for half in range(2)]
        return jnp.concatenate(parts, axis=1).astype(BF16)

    s_parts = [lax.dot_general(q, page(kc), _NT, preferred_element_type=F32) for kc in kc_refs]
    update(jnp.concatenate(s_parts, axis=-1), [page(vc) for vc in vc_refs])

    @pl.when(step == n_steps - 1)
    def _():
        lam = _lambda(lam_ref, lam_init)
        hw = 2 * DH
        for h in range(H_B):
            r0 = h * 2 * n_new
            a0 = acc_ref[r0:r0 + n_new, h * hw:(h + 1) * hw] / l_ref[r0:r0 + n_new, 0:1]
            a1 = acc_ref[r0 + n_new:r0 + 2 * n_new, h * hw:(h + 1) * hw] / l_ref[r0 + n_new:r0 + 2 * n_new, 0:1]
            o = a0 - lam * a1
            o_ref[0, :, h * hw:(h + 1) * hw] = (_rms(o, sg_ref[...]) * (1.0 - lam_init)).astype(BF16)


def _attn_sample(qbd, k_new, v_new, cache_k, cache_v, page_ids, lam_rows, subln, lam_init, *, n_pages, group):
    batch, rows, width = qbd.shape
    n_new = rows // (2 * H_B)
    n_steps = n_pages // group
    hw = 2 * DH
    page_rows = PAGE_SIZE * 2 * H_B

    def cache_spec(g):
        return pl.BlockSpec((page_rows, DH), lambda b, s, pt: (pt[b * n_pages + s * group + g], 0))

    in_specs = [
        pl.BlockSpec((8, DH), lambda b, s, pt: (0, 0)),
        pl.BlockSpec((1, hw), lambda b, s, pt: (0, 0)),
        pl.BlockSpec((1, rows, width), lambda b, s, pt: (b, 0, 0)),
        pl.BlockSpec((1, k_new.shape[1], width), lambda b, s, pt: (b, 0, 0)),
        pl.BlockSpec((1, v_new.shape[1], width), lambda b, s, pt: (b, 0, 0)),
    ] + [cache_spec(g) for g in range(group)] + [cache_spec(g) for g in range(group)]
    grid_spec = pltpu.PrefetchScalarGridSpec(
        num_scalar_prefetch=1,
        grid=(batch, n_steps),
        in_specs=in_specs,
        out_specs=pl.BlockSpec((1, n_new, width), lambda b, s, pt: (b, 0, 0)),
        scratch_shapes=[
            pltpu.VMEM((rows, LANES), F32),
            pltpu.VMEM((rows, LANES), F32),
            pltpu.VMEM((rows, width), F32),
        ],
    )
    return pl.pallas_call(
        functools.partial(_attn_sample_body, group=group, n_steps=n_steps, n_new=n_new, lam_init=lam_init),
        grid_spec=grid_spec,
        out_shape=jax.ShapeDtypeStruct((batch, n_new, width), BF16),
        compiler_params=_params(("parallel", "arbitrary"), 48),
        name="attn_sample",
    )(page_ids, lam_rows, subln.reshape(1, hw), qbd, k_new, v_new,
      *([cache_k] * group), *([cache_v] * group))


def _outproj_body(h_ref, a_ref, b_ref, wa_ref, wb_ref, o_ref):
    mix = jnp.dot(a_ref[...], wa_ref[...], preferred_element_type=F32)
    mix = mix + jnp.dot(b_ref[...], wb_ref[...], preferred_element_type=F32)
    o_ref[...] = h_ref[...] + mix


def _outproj(h, oa, ob, wa, wb, *, tm, tn=512):
    t, d = h.shape
    return pl.pallas_call(
        _outproj_body,
        grid=(t // tm, d // tn),
        in_specs=[
            pl.BlockSpec((tm, tn), lambda i, j: (i, j)),
            pl.BlockSpec((tm, D_A), lambda i, j: (i, 0)),
            pl.BlockSpec((tm, D_B), lambda i, j: (i, 0)),
            pl.BlockSpec((D_A, tn), lambda i, j: (0, j)),
            pl.BlockSpec((D_B, tn), lambda i, j: (0, j)),
        ],
        out_specs=pl.BlockSpec((tm, tn), lambda i, j: (i, j)),
        out_shape=jax.ShapeDtypeStruct((t, d), F32),
        compiler_params=_params(("parallel", "arbitrary"), 40),
        name="outproj",
    )(h, oa, ob, wa, wb)


def _layer(x, conv_buf, s_delta, attend, w, *, batch, seq, tm, gdn_tb, v_block=None):
    h, hn = _ffn(x, w["ffn1_norm"], w["ffn1_wg"], w["ffn1_wu"], w["ffn1_wd"], w["mix_norm"], tm=tm)
    ptm = min(2 * tm, x.shape[0])
    (ua,) = _proj(hn, w["w_qkvz"], "plain", tm=ptm, tn=512)
    (gates,) = _proj(hn, w["w_gate"], "plain", tm=ptm, tn=LANES)
    (qb,) = _proj(hn, w["w_qb"], "qnorm", w["q_norm"], tm=ptm, tn=512)
    kb32, kb16 = _proj(hn, w["w_kb"], "knorm", w["k_norm"], tm=ptm, tn=D_B)
    if v_block is None:
        vb32, vb16 = _proj(hn, w["w_vb"], "dual", tm=ptm, tn=D_B)
    else:
        vb32, vb16 = _proj(hn, w["w_vb"], "dual_t", tm=ptm, tn=D_B, tk=v_block)

    buf8 = jnp.pad(conv_buf, ((0, 0), (8 - (CONV_W - 1), 0), (0, 0)))
    oa, new_s = _gdn(ua, gates, buf8, w["conv_w"], w["A_log"], w["dt_bias"], w["gdn_out_norm"], s_delta,
                     batch=batch, seq=seq, tb=gdn_tb)
    assert seq >= CONV_W - 1
    new_buf = ua.reshape(batch, seq, 4 * D_A)[:, seq - (CONV_W - 1):, :3 * D_A]

    ob = attend(qb, kb16, vb16)
    h2 = _outproj(h, oa, ob, w["w_out_a"], w["w_out_b"], tm=ptm)
    y = _ffn(h2, w["ffn2_norm"], w["ffn2_wg"], w["ffn2_wu"], w["ffn2_wd"], tm=tm)
    return y, kb32, vb32, new_s, new_buf


def kernel(x_prompt, x_sample, cache_k, cache_v, page_table, state_delta, state_conv, ffn1_norm, ffn1_w_gate, ffn1_w_up, ffn1_w_down, mix_norm, w_in, conv_w, A_log, dt_bias, gdn_out_norm, q_norm, k_norm, lambda_q1, lambda_k1, lambda_q2, lambda_k2, diff_subln, w_out, ffn2_norm, ffn2_w_gate, ffn2_w_up, ffn2_w_down):
    depth = ffn1_norm.shape[0]
    bp, sp, _ = x_prompt.shape
    assert bp == 1
    bs, ss, _ = x_sample.shape
    n_pages = page_table.shape[1]
    n_pool = cache_k.shape[1]
    hw = 2 * DH
    width = H_B * hw

    yp = x_prompt.reshape(bp * sp, D_MODEL)
    ys = x_sample.reshape(bs * ss, D_MODEL)
    def page_rows(cache):
        x = cache.reshape(depth * n_pool, PAGE_SIZE, H_B, 2, DH)
        return jnp.transpose(x, (0, 1, 3, 2, 4)).reshape(depth * n_pool * PAGE_SIZE * 2 * H_B, DH)

    cache_k2 = page_rows(cache_k)
    cache_v2 = page_rows(cache_v)
    outs = [[] for _ in range(8)]
    for l in range(depth):
        lam_init = 0.8 - 0.6 * math.exp(-0.3 * l)
        wl = w_in[l]
        small = jnp.zeros((D_MODEL, LANES), F32).at[:, :2 * H_A].set(wl[:, 4 * D_A:4 * D_A + 2 * H_A])
        off = 4 * D_A + 2 * H_A
        w = {
            "ffn1_norm": ffn1_norm[l], "mix_norm": mix_norm[l], "ffn2_norm": ffn2_norm[l],
            "ffn1_wg": ffn1_w_gate[l].astype(BF16), "ffn1_wu": ffn1_w_up[l].astype(BF16),
            "ffn1_wd": ffn1_w_down[l].astype(BF16),
            "ffn2_wg": ffn2_w_gate[l].astype(BF16), "ffn2_wu": ffn2_w_up[l].astype(BF16),
            "ffn2_wd": ffn2_w_down[l].astype(BF16),
            "w_qkvz": wl[:, :4 * D_A].astype(BF16), "w_gate": small.astype(BF16),
            "w_qb": wl[:, off:off + D_B].astype(BF16), "w_kb": wl[:, off + D_B:off + 2 * D_B].astype(BF16),
            "w_vb": wl[:, off + 2 * D_B:off + 3 * D_B].astype(BF16),
            "w_out_a": w_out[l, :D_A].astype(BF16), "w_out_b": w_out[l, D_A:].astype(BF16),
            "conv_w": conv_w[l], "A_log": A_log[l], "dt_bias": dt_bias[l], "gdn_out_norm": gdn_out_norm[l],
            "q_norm": q_norm[l], "k_norm": k_norm[l],
        }
        lam_rows = jnp.zeros((8, DH), F32)
        lam_rows = lam_rows.at[0].set(lambda_q1[l]).at[1].set(lambda_k1[l]).at[2].set(lambda_q2[l]).at[3].set(lambda_k2[l])
        subln = diff_subln[l]

        tq = 512

        def attend_prompt(q, k, vt):
            return _attn_prompt(q, k, vt, lam_rows, subln, lam_init, tq=tq)

        def attend_sample(q, k, v):
            q4 = q.reshape(bs, ss, 2 * H_B, DH)
            eye = jnp.eye(2 * H_B, dtype=BF16)
            qbd = (q4[:, :, :, None, :] * eye[None, None, :, :, None])
            qbd = jnp.transpose(qbd, (0, 2, 1, 3, 4)).reshape(bs, 2 * H_B * ss, width)
            k_new = jnp.pad(k.reshape(bs, ss, width), ((0, 0), (0, 16 - ss), (0, 0)))
            v_new = jnp.pad(v.reshape(bs, ss, width), ((0, 0), (0, 16 - ss), (0, 0)))
            page_ids = (page_table + l * n_pool).reshape(-1).astype(jnp.int32)
            o = _attn_sample(qbd, k_new, v_new, cache_k2, cache_v2, page_ids, lam_rows, subln, lam_init,
                             n_pages=n_pages, group=16)
            return o.reshape(bs * ss, width)

        buf0 = jnp.zeros((bp, CONV_W - 1, 3 * D_A), F32)
        s0 = jnp.zeros((bp, H_A, DH, DH), F32)
        yp, kp, vp, s_p, b_p = _layer(yp, buf0, s0, attend_prompt, w, batch=bp, seq=sp, tm=512,
                                      gdn_tb=128, v_block=tq)
        ys, k_s, v_s, s_s, b_s = _layer(ys, state_conv[l], state_delta[l], attend_sample, w, batch=bs, seq=ss,
                                        tm=bs * ss, gdn_tb=ss)
        def cache_view(rows, b, s):
            return jnp.transpose(rows.reshape(b, s, 2, H_B, DH), (0, 1, 3, 2, 4)).reshape(b, s, H_B, hw)

        for lst, val in zip(outs, (cache_view(kp, bp, sp), cache_view(vp, bp, sp),
                                   cache_view(k_s, bs, ss), cache_view(v_s, bs, ss),
                                   s_p, b_p, s_s, b_s)):
            lst.append(val)
    return (yp.reshape(bp, sp, D_MODEL), ys.reshape(bs, ss, D_MODEL), *[jnp.stack(o) for o in outs])
```

```python
import functools
import math

import jax
import jax.numpy as jnp
from jax import lax
from jax.experimental import pallas as pl
from jax.experimental.pallas import tpu as pltpu

F32 = jnp.float32
BF16 = jnp.bfloat16

D_MODEL = 2048
D_A = 1024
D_B = 1024
DH = 128
H_A = D_A // DH
H_B = D_B // (2 * DH)
CONV_W = 4
D_FF = 5632
EPS = 1e-6
PAGE_SIZE = 128
LANES = 128
BF16_ROWS = 16
LOG2E = math.log2(math.e)
GATE_LANE = 8
TRI_BASE = 16
CONV_HIST = BF16_ROWS

MIB = 1024 * 1024


def _params(semantics, vmem_mib):
    return pltpu.CompilerParams(dimension_semantics=semantics, vmem_limit_bytes=vmem_mib * MIB)


def _rms(x, g):
    return x * lax.rsqrt(jnp.mean(x * x, axis=-1, keepdims=True) + EPS) * g


def _silu(x):
    return x * jax.nn.sigmoid(x)


def _softplus(x):
    return jnp.maximum(x, 0.0) + jnp.log1p(jnp.exp(-jnp.abs(x)))


def _ffn_body(x_ref, g_ref, wg_ref, wu_ref, wd_ref, *rest, nf, with_norm):
    if with_norm:
        ng_ref, o_ref, on_ref, xn_ref = rest
    else:
        o_ref, xn_ref = rest
    f = pl.program_id(1)

    @pl.when(f == 0)
    def _():
        xn_ref[...] = _rms(x_ref[...], g_ref[...]).astype(BF16)
        o_ref[...] = jnp.zeros_like(o_ref)

    xn = xn_ref[...]
    a = jnp.dot(xn, wg_ref[...], preferred_element_type=F32)
    b = jnp.dot(xn, wu_ref[...], preferred_element_type=F32)
    hid = (_silu(a) * b).astype(BF16)
    o_ref[...] += jnp.dot(hid, wd_ref[...], preferred_element_type=F32)

    @pl.when(f == nf - 1)
    def _():
        y = x_ref[...] + 0.5 * o_ref[...]
        o_ref[...] = y
        if with_norm:
            on_ref[...] = _rms(y, ng_ref[...]).astype(BF16)


def _ffn(x, g, wg, wu, wd, next_g=None, *, tm, tf=512):
    t, d = x.shape
    nf = D_FF // tf
    with_norm = next_g is not None
    in_specs = [
        pl.BlockSpec((tm, d), lambda i, f: (i, 0)),
        pl.BlockSpec((1, d), lambda i, f: (0, 0)),
        pl.BlockSpec((d, tf), lambda i, f: (0, f)),
        pl.BlockSpec((d, tf), lambda i, f: (0, f)),
        pl.BlockSpec((tf, d), lambda i, f: (f, 0)),
    ]
    args = [x, g.reshape(1, d), wg, wu, wd]
    out_shape = [jax.ShapeDtypeStruct((t, d), F32)]
    out_specs = [pl.BlockSpec((tm, d), lambda i, f: (i, 0))]
    if with_norm:
        in_specs.append(pl.BlockSpec((1, d), lambda i, f: (0, 0)))
        args.append(next_g.reshape(1, d))
        out_shape.append(jax.ShapeDtypeStruct((t, d), BF16))
        out_specs.append(pl.BlockSpec((tm, d), lambda i, f: (i, 0)))
    res = pl.pallas_call(
        functools.partial(_ffn_body, nf=nf, with_norm=with_norm),
        grid=(t // tm, nf),
        in_specs=in_specs,
        out_specs=out_specs,
        out_shape=out_shape,
        scratch_shapes=[pltpu.VMEM((tm, d), BF16)],
        compiler_params=_params(("parallel", "arbitrary"), 48),
        name="ffn",
    )(*args)
    return res if with_norm else res[0]


def _store_cache_rows(o_ref, y, c):
    head, half = divmod(c, 2)
    o_ref[pl.ds(half * H_B + head, y.shape[0], stride=2 * H_B), :] = y


def _proj_body(x_ref, w_ref, *rest, mode):
    acc = jnp.dot(x_ref[...], w_ref[...], preferred_element_type=F32)
    groups = [(c, slice(c * DH, (c + 1) * DH)) for c in range(acc.shape[1] // DH)]
    if mode == "plain":
        (o_ref,) = rest
        o_ref[...] = acc
    elif mode == "qnorm":
        g_ref, o_ref = rest
        for c, cols in groups:
            o_ref[:, cols] = (_rms(acc[:, cols], g_ref[...]) * (LOG2E * DH ** -0.5)).astype(BF16)
    elif mode == "knorm":
        g_ref, o32_ref, o16_ref = rest
        for c, cols in groups:
            y = _rms(acc[:, cols], g_ref[...])
            _store_cache_rows(o32_ref, y, c)
            o16_ref[:, cols] = y.astype(BF16)
    else:
        o32_ref, o16_ref = rest
        for c, cols in groups:
            _store_cache_rows(o32_ref, acc[:, cols], c)
        if mode == "dual":
            o16_ref[...] = acc.astype(BF16)
        else:
            tk = o16_ref.shape[2]
            for r in range(o16_ref.shape[0]):
                o16_ref[r] = acc[r * tk:(r + 1) * tk, :].T.astype(BF16)


def _proj(xn, w, mode, gain=None, *, tm, tn, tk=None):
    t, d = xn.shape
    n = w.shape[1]
    in_specs = [pl.BlockSpec((tm, d), lambda i, j: (i, 0)), pl.BlockSpec((d, tn), lambda i, j: (0, j))]
    args = [xn, w]
    if mode in ("qnorm", "knorm"):
        in_specs.append(pl.BlockSpec((1, DH), lambda i, j: (0, 0)))
        args.append(gain.reshape(1, DH))
    o_spec = pl.BlockSpec((tm, tn), lambda i, j: (i, j))
    if mode == "plain":
        out_shape, out_specs = [jax.ShapeDtypeStruct((t, n), F32)], [o_spec]
    elif mode == "qnorm":
        out_shape, out_specs = [jax.ShapeDtypeStruct((t, n), BF16)], [o_spec]
    else:
        assert tn == n == H_B * 2 * DH
        rows = n // DH
        out_shape = [jax.ShapeDtypeStruct((t * rows, DH), F32)]
        out_specs = [pl.BlockSpec((tm * rows, DH), lambda i, j: (i, 0))]
        if mode == "dual_t":
            out_shape.append(jax.ShapeDtypeStruct((t // tk, n, tk), BF16))
            out_specs.append(pl.BlockSpec((tm // tk, tn, tk), lambda i, j: (i, j, 0)))
        else:
            out_shape.append(jax.ShapeDtypeStruct((t, n), BF16))
            out_specs.append(o_spec)
    return pl.pallas_call(
        functools.partial(_proj_body, mode=mode),
        grid=(t // tm, n // tn),
        in_specs=in_specs,
        out_specs=out_specs,
        out_shape=out_shape,
        compiler_params=_params(("parallel", "arbitrary"), 40),
        name="proj_" + mode,
    )(*args)


def _split(x):
    hi = x.astype(BF16)
    lo = (x - hi.astype(F32)).astype(BF16)
    return hi, lo


def _dot3(a, b, dims=(((1,), (0,)), ((), ()))):
    ah, al = _split(a)
    bh, bl = _split(b)
    dg = functools.partial(lax.dot_general, dimension_numbers=dims, preferred_element_type=F32)
    return dg(ah, bh) + (dg(ah, bl) + dg(al, bh))


_NT = (((1,), (1,)), ((), ()))
_TN = (((0,), (0,)), ((), ()))


_B_NN = (((2,), (1,)), ((0,), (0,)))
_B_NT = (((2,), (2,)), ((0,), (0,)))
_B_TN = (((1,), (1,)), ((0,), (0,)))


def _bdot(a, b, dims=_B_NN):
    return lax.dot_general(a.astype(BF16), b.astype(BF16), dims, preferred_element_type=F32)


def _tri_inv(a):
    c = a.shape[-1]
    base = min(TRI_BASE, c)
    row = lax.broadcasted_iota(jnp.int32, (c, c), 0)
    col = lax.broadcasted_iota(jnp.int32, (c, c), 1)
    eye = (row == col).astype(F32)

    def neumann(m, order):
        x = eye - m
        p = m
        k = 1
        while 2 * k < order:
            p = _bdot(p, p)
            x = x + _bdot(x, p)
            k *= 2
        return x

    if c == base:
        return neumann(a, base)
    shift = base.bit_length() - 1
    same = (row >> shift) == (col >> shift)
    dinv = neumann(jnp.where(same, a, 0.0), base)
    e = _bdot(dinv, jnp.where(same, 0.0, a))
    return _bdot(neumann(e, c // base), dinv)


def _gdn_body(qkv_ref, z_ref, gt_ref, buf_ref, cw_ref, ad_ref, gn_ref, s0_ref,
              o_ref, s_ref, xp_ref, *, tb):
    t = pl.program_id(1)

    @pl.when(t == 0)
    def _():
        s_ref[...] = s0_ref[...]
        xp_ref[0:CONV_HIST, :] = buf_ref[0]

    x = qkv_ref[...]
    xp_ref[CONV_HIST:CONV_HIST + tb, :] = x
    if tb >= LANES:
        r = lax.broadcasted_iota(jnp.int32, ((CONV_W - 1) * tb, tb + CONV_HIST), 0)
        c = lax.broadcasted_iota(jnp.int32, ((CONV_W - 1) * tb, tb + CONV_HIST), 1)
        shift = 1 + lax.shift_right_logical(r, tb.bit_length() - 1)
        sel = (c == CONV_HIST + (r & (tb - 1)) - shift).astype(BF16)
        prev = jnp.dot(sel, xp_ref[...].astype(BF16), preferred_element_type=F32)
        y = cw_ref[0:1, :] * prev[2 * tb:3 * tb]
        y = y + cw_ref[1:2, :] * prev[tb:2 * tb]
        y = y + cw_ref[2:3, :] * prev[0:tb]
    else:
        y = cw_ref[0:1, :] * xp_ref[CONV_HIST - 3:CONV_HIST - 3 + tb, :]
        y = y + cw_ref[1:2, :] * xp_ref[CONV_HIST - 2:CONV_HIST - 2 + tb, :]
        y = y + cw_ref[2:3, :] * xp_ref[CONV_HIST - 1:CONV_HIST - 1 + tb, :]
    y = _silu(y + cw_ref[3:4, :] * x)
    xp_ref[0:CONV_HIST, :] = xp_ref[tb:tb + CONV_HIST, :]

    def heads(base):
        return jnp.stack([y[:, base + h * DH:base + (h + 1) * DH] for h in range(H_A)])

    def l2(x):
        return x * lax.rsqrt(jnp.sum(x * x, axis=-1, keepdims=True) + EPS)

    q = l2(heads(0)) * (DH ** -0.5)
    k = l2(heads(D_A))
    v = heads(2 * D_A)

    gt = gt_ref[...]
    beta = jax.nn.sigmoid(gt)
    g = -jnp.exp(ad_ref[0:1, :]) * _softplus(gt + ad_ref[1:2, :])
    row = lax.broadcasted_iota(jnp.int32, (tb, tb), 0)
    col = lax.broadcasted_iota(jnp.int32, (tb, tb), 1)
    tri_incl = col <= row
    tri_strict = col < row
    gc = _dot3(tri_incl.astype(F32), g)
    pad = max(LANES - tb, 0)
    gc_rows = jnp.concatenate([gc, jnp.zeros((pad, LANES), F32)], axis=0) if pad else gc
    gct = gc_rows.T

    bcol = jnp.stack([beta[:, h:h + 1] for h in range(H_A)])
    gcol = jnp.stack([gc[:, GATE_LANE + h:GATE_LANE + h + 1] for h in range(H_A)])
    grow = jnp.stack([gct[GATE_LANE + h:GATE_LANE + h + 1, 0:tb] for h in range(H_A)])
    decay = jnp.where(tri_incl, jnp.exp(jnp.where(tri_incl, gcol - grow, 0.0)), 0.0)
    egc = jnp.exp(gcol)
    kb = k * bcol

    amat = jnp.where(tri_strict, _bdot(kb, k, _B_NT) * decay, 0.0)
    tinv = _tri_inv(amat)
    uw = _bdot(tinv, jnp.concatenate([v * bcol, kb * egc], axis=-1))
    u, w = uw[:, :, :DH], uw[:, :, DH:]
    qk = jnp.where(tri_incl, _bdot(q, k, _B_NT) * decay, 0.0)

    s = s_ref[0]
    wq_s = _bdot(jnp.concatenate([w, q * egc], axis=1), s)
    v_new = u - wq_s[:, :tb]
    o = wq_s[:, tb:] + _bdot(qk, v_new)
    g_last = gcol[:, tb - 1:tb, :]
    k_dec = k * jnp.exp(g_last - gcol)
    s_ref[0] = s * jnp.exp(g_last) + _bdot(k_dec, v_new, _B_TN)

    on = _rms(o, gn_ref[...])
    for h in range(H_A):
        gate = _silu(z_ref[:, h * DH:(h + 1) * DH])
        o_ref[:, h * DH:(h + 1) * DH] = (on[h] * gate).astype(BF16)


def _gdn(ua, gates, hist, conv_w, a_log, dt_bias, out_norm, s0, *, batch, seq, tb):
    nt = seq // tb
    ad = jnp.zeros((8, LANES), F32)
    ad = ad.at[0, GATE_LANE:GATE_LANE + H_A].set(a_log).at[1, GATE_LANE:GATE_LANE + H_A].set(dt_bias)
    cw = jnp.zeros((8, 3 * D_A), F32).at[:CONV_W].set(conv_w)
    return pl.pallas_call(
        functools.partial(_gdn_body, tb=tb),
        grid=(batch, nt),
        in_specs=[
            pl.BlockSpec((tb, 3 * D_A), lambda b, t: (b * nt + t, 0)),
            pl.BlockSpec((tb, D_A), lambda b, t: (b * nt + t, 3)),
            pl.BlockSpec((tb, LANES), lambda b, t: (b * nt + t, 0)),
            pl.BlockSpec((1, CONV_HIST, 3 * D_A), lambda b, t: (b, 0, 0)),
            pl.BlockSpec((8, 3 * D_A), lambda b, t: (0, 0)),
            pl.BlockSpec((8, LANES), lambda b, t: (0, 0)),
            pl.BlockSpec((1, DH), lambda b, t: (0, 0)),
            pl.BlockSpec((1, H_A, DH, DH), lambda b, t: (b, 0, 0, 0)),
        ],
        out_specs=[
            pl.BlockSpec((tb, D_A), lambda b, t: (b * nt + t, 0)),
            pl.BlockSpec((1, H_A, DH, DH), lambda b, t: (b, 0, 0, 0)),
        ],
        out_shape=[
            jax.ShapeDtypeStruct((batch * seq, D_A), BF16),
            jax.ShapeDtypeStruct((batch, H_A, DH, DH), F32),
        ],
        scratch_shapes=[pltpu.VMEM((tb + CONV_HIST, 3 * D_A), F32)],
        compiler_params=_params(("parallel", "arbitrary"), 40),
        name="gdn",
    )(ua, ua, gates, hist, cw, ad, out_norm.reshape(1, DH), s0)


def _lambda(lam_ref, lam_init):
    l1 = jnp.sum(lam_ref[0:1, :] * lam_ref[1:2, :], axis=-1, keepdims=True)
    l2 = jnp.sum(lam_ref[2:3, :] * lam_ref[3:4, :], axis=-1, keepdims=True)
    return jnp.exp(l1) - jnp.exp(l2) + lam_init


def _attn_prompt_body(lam_ref, sg_ref, q_ref, k_ref, vt_ref, o_ref, acc_ref, s_ref, *, tq, lam_init):
    qi = pl.program_id(1)
    hw = 2 * DH
    acc_ref[...] = jnp.zeros(acc_ref.shape, F32)
    ones = jnp.ones((acc_ref.shape[1] - hw, tq), BF16)

    def scores(j, sub):
        cols = slice(sub * DH, (sub + 1) * DH)
        ks = k_ref[pl.ds(pl.multiple_of(j * tq, tq), tq), cols]
        return lax.dot_general(ks, q_ref[:, cols], _NT, preferred_element_type=F32)

    def accumulate(j, sub, st, m_prev, masked=False):
        if masked:
            key = lax.broadcasted_iota(jnp.int32, (tq, tq), 0)
            qry = lax.broadcasted_iota(jnp.int32, (tq, tq), 1)
            st = jnp.where(key <= qry, st, -jnp.inf)
        vt = jnp.concatenate([vt_ref[j], ones], axis=0)
        m_new = jnp.maximum(m_prev, jnp.max(st, axis=0, keepdims=True))
        corr = jnp.exp2(m_prev - m_new)
        p = jnp.exp2(st - m_new).astype(BF16)
        acc_ref[sub] = acc_ref[sub] * corr + jnp.dot(vt, p, preferred_element_type=F32)
        return m_new

    s_ref[0] = scores(0, 0)
    s_ref[1] = scores(0, 1)

    def pair(p, maxes):
        a, b = 2 * p, 2 * p + 1
        st_b0 = scores(b, 0)
        m0 = accumulate(a, 0, s_ref[0], maxes[0])
        st_b1 = scores(b, 1)
        m1 = accumulate(a, 1, s_ref[1], maxes[1])
        s_ref[0] = scores(b + 1, 0)
        m0 = accumulate(b, 0, st_b0, m0)
        s_ref[1] = scores(b + 1, 1)
        return m0, accumulate(b, 1, st_b1, m1)

    init = jnp.full((1, tq), -jnp.inf, F32)
    n_pairs = lax.shift_right_logical(qi, 1)
    maxes = lax.fori_loop(0, n_pairs, pair, (init, init))

    @pl.when((qi & 1) == 1)
    def _():
        st_d0 = scores(qi, 0)
        m0 = accumulate(qi - 1, 0, s_ref[0], maxes[0])
        st_d1 = scores(qi, 1)
        m1 = accumulate(qi - 1, 1, s_ref[1], maxes[1])
        accumulate(qi, 0, st_d0, m0, masked=True)
        accumulate(qi, 1, st_d1, m1, masked=True)

    @pl.when((qi & 1) == 0)
    def _():
        accumulate(qi, 0, s_ref[0], maxes[0], masked=True)
        accumulate(qi, 1, s_ref[1], maxes[1], masked=True)

    lam = _lambda(lam_ref, lam_init)
    a0, a1 = acc_ref[0], acc_ref[1]
    ot = a0[:hw] / a0[hw:hw + 1] - lam * (a1[:hw] / a1[hw:hw + 1])
    o_ref[...] = (_rms(ot.T, sg_ref[...]) * (1.0 - lam_init)).astype(BF16)


def _attn_prompt(q, k, vt, lam_rows, subln, lam_init, *, tq):
    s_len = q.shape[0]
    hw = 2 * DH
    return pl.pallas_call(
        functools.partial(_attn_prompt_body, tq=tq, lam_init=lam_init),
        grid=(H_B, s_len // tq),
        in_specs=[
            pl.BlockSpec((8, DH), lambda h, i: (0, 0)),
            pl.BlockSpec((1, hw), lambda h, i: (0, 0)),
            pl.BlockSpec((tq, hw), lambda h, i: (i, h)),
            pl.BlockSpec((s_len, hw), lambda h, i: (0, h)),
            pl.BlockSpec((s_len // tq, hw, tq), lambda h, i: (0, h, 0)),
        ],
        out_specs=pl.BlockSpec((tq, hw), lambda h, i: (i, h)),
        out_shape=jax.ShapeDtypeStruct((s_len, H_B * hw), BF16),
        scratch_shapes=[pltpu.VMEM((2, hw + BF16_ROWS, tq), F32), pltpu.VMEM((2, tq, tq), F32)],
        compiler_params=_params(("parallel", "arbitrary"), 52),
        name="attn_prompt",
    )(lam_rows, subln.reshape(1, hw), q, k, vt)


def _attn_sample_body(pt_ref, lam_ref, sg_ref, q_ref, kn_ref, vn_ref, *rest, group, n_steps, n_new, lam_init):
    kc_refs = rest[:group]
    vc_refs = rest[group:2 * group]
    o_ref, m_ref, l_ref, acc_ref = rest[2 * group:]
    step = pl.program_id(1)
    rows = q_ref.shape[1]
    q = q_ref[0]

    def update(s, vals):
        m_prev = m_ref[:, 0:1]
        m_new = jnp.maximum(m_prev, jnp.max(s, axis=-1, keepdims=True))
        corr = jnp.exp2(m_prev - m_new)
        p = jnp.exp2(s - m_new)
        l_new = corr * l_ref[:, 0:1] + jnp.sum(p, axis=-1, keepdims=True)
        pb = p.astype(BF16)
        pv = None
        width = s.shape[1] // len(vals)
        for i, val in enumerate(vals):
            part = jnp.dot(pb[:, i * width:(i + 1) * width], val, preferred_element_type=F32)
            pv = part if pv is None else pv + part
        acc_ref[...] = acc_ref[...] * corr + pv
        m_ref[...] = jnp.broadcast_to(m_new, m_ref.shape)
        l_ref[...] = jnp.broadcast_to(l_new, l_ref.shape)

    @pl.when(step == 0)
    def _():
        m_ref[...] = jnp.full(m_ref.shape, -jnp.inf, F32)
        l_ref[...] = jnp.zeros(l_ref.shape, F32)
        acc_ref[...] = jnp.zeros(acc_ref.shape, F32)
        kn = kn_ref[0]
        s = lax.dot_general(q, kn, _NT, preferred_element_type=F32)
        tok = lax.broadcasted_iota(jnp.int32, s.shape, 0) & (n_new - 1)
        key = lax.broadcasted_iota(jnp.int32, s.shape, 1)
        s = jnp.where(key <= tok, s, -jnp.inf)
        update(s, [vn_ref[0]])

    def page(ref):
        parts = [ref[pl.ds(half * H_B + h, PAGE_SIZE, stride=2 * H_B), :] for h in range(H_B) for half in range(2)]
        return jnp.concatenate(parts, axis=1).astype(BF16)

    s_parts = [lax.dot_general(q, page(kc), _NT, preferred_element_type=F32) for kc in kc_refs]
    update(jnp.concatenate(s_parts, axis=-1), [page(vc) for vc in vc_refs])

    @pl.when(step == n_steps - 1)
    def _():
        lam = _lambda(lam_ref, lam_init)
        hw = 2 * DH
        for h in range(H_B):
            r0 = h * 2 * n_new
            a0 = acc_ref[r0:r0 + n_new, h * hw:(h + 1) * hw] / l_ref[r0:r0 + n_new, 0:1]
            a1 = acc_ref[r0 + n_new:r0 + 2 * n_new, h * hw:(h + 1) * hw] / l_ref[r0 + n_new:r0 + 2 * n_new, 0:1]
            o = a0 - lam * a1
            o_ref[0, :, h * hw:(h + 1) * hw] = (_rms(o, sg_ref[...]) * (1.0 - lam_init)).astype(BF16)


def _attn_sample(qbd, k_new, v_new, cache_k, cache_v, page_ids, lam_rows, subln, lam_init, *, n_pages, group):
    batch, rows, width = qbd.shape
    n_new = rows // (2 * H_B)
    n_steps = n_pages // group
    hw = 2 * DH
    page_rows = PAGE_SIZE * 2 * H_B

    def cache_spec(g):
        return pl.BlockSpec((page_rows, DH), lambda b, s, pt: (pt[b * n_pages + s * group + g], 0))

    in_specs = [
        pl.BlockSpec((8, DH), lambda b, s, pt: (0, 0)),
        pl.BlockSpec((1, hw), lambda b, s, pt: (0, 0)),
        pl.BlockSpec((1, rows, width), lambda b, s, pt: (b, 0, 0)),
        pl.BlockSpec((1, k_new.shape[1], width), lambda b, s, pt: (b, 0, 0)),
        pl.BlockSpec((1, v_new.shape[1], width), lambda b, s, pt: (b, 0, 0)),
    ] + [cache_spec(g) for g in range(group)] + [cache_spec(g) for g in range(group)]
    grid_spec = pltpu.PrefetchScalarGridSpec(
        num_scalar_prefetch=1,
        grid=(batch, n_steps),
        in_specs=in_specs,
        out_specs=pl.BlockSpec((1, n_new, width), lambda b, s, pt: (b, 0, 0)),
        scratch_shapes=[
            pltpu.VMEM((rows, LANES), F32),
            pltpu.VMEM((rows, LANES), F32),
            pltpu.VMEM((rows, width), F32),
        ],
    )
    return pl.pallas_call(
        functools.partial(_attn_sample_body, group=group, n_steps=n_steps, n_new=n_new, lam_init=lam_init),
        grid_spec=grid_spec,
        out_shape=jax.ShapeDtypeStruct((batch, n_new, width), BF16),
        compiler_params=_params(("parallel", "arbitrary"), 48),
        name="attn_sample",
    )(page_ids, lam_rows, subln.reshape(1, hw), qbd, k_new, v_new,
      *([cache_k] * group), *([cache_v] * group))


def _outproj_body(h_ref, a_ref, b_ref, wa_ref, wb_ref, o_ref):
    mix = jnp.dot(a_ref[...], wa_ref[...], preferred_element_type=F32)
    mix = mix + jnp.dot(b_ref[...], wb_ref[...], preferred_element_type=F32)
    o_ref[...] = h_ref[...] + mix


def _outproj(h, oa, ob, wa, wb, *, tm, tn=512):
    t, d = h.shape
    return pl.pallas_call(
        _outproj_body,
        grid=(t // tm, d // tn),
        in_specs=[
            pl.BlockSpec((tm, tn), lambda i, j: (i, j)),
            pl.BlockSpec((tm, D_A), lambda i, j: (i, 0)),
            pl.BlockSpec((tm, D_B), lambda i, j: (i, 0)),
            pl.BlockSpec((D_A, tn), lambda i, j: (0, j)),
            pl.BlockSpec((D_B, tn), lambda i, j: (0, j)),
        ],
        out_specs=pl.BlockSpec((tm, tn), lambda i, j: (i, j)),
        out_shape=jax.ShapeDtypeStruct((t, d), F32),
        compiler_params=_params(("parallel", "arbitrary"), 40),
        name="outproj",
    )(h, oa, ob, wa, wb)


def _layer(x, conv_buf, s_delta, attend, w, *, batch, seq, tm, gdn_tb, v_block=None):
    h, hn = _ffn(x, w["ffn1_norm"], w["ffn1_wg"], w["ffn1_wu"], w["ffn1_wd"], w["mix_norm"], tm=tm)
    ptm = min(2 * tm, x.shape[0])
    (ua,) = _proj(hn, w["w_qkvz"], "plain", tm=ptm, tn=512)
    (gates,) = _proj(hn, w["w_gate"], "plain", tm=ptm, tn=LANES)
    (qb,) = _proj(hn, w["w_qb"], "qnorm", w["q_norm"], tm=ptm, tn=512)
    kb32, kb16 = _proj(hn, w["w_kb"], "knorm", w["k_norm"], tm=ptm, tn=D_B)
    if v_block is None:
        vb32, vb16 = _proj(hn, w["w_vb"], "dual", tm=ptm, tn=D_B)
    else:
        vb32, vb16 = _proj(hn, w["w_vb"], "dual_t", tm=ptm, tn=D_B, tk=v_block)

    hist = jnp.pad(conv_buf, ((0, 0), (CONV_HIST - (CONV_W - 1), 0), (0, 0)))
    oa, new_s = _gdn(ua, gates, hist, w["conv_w"], w["A_log"], w["dt_bias"], w["gdn_out_norm"], s_delta,
                     batch=batch, seq=seq, tb=gdn_tb)
    assert seq >= CONV_W - 1
    new_buf = ua.reshape(batch, seq, 4 * D_A)[:, seq - (CONV_W - 1):, :3 * D_A]

    ob = attend(qb, kb16, vb16)
    h2 = _outproj(h, oa, ob, w["w_out_a"], w["w_out_b"], tm=ptm)
    y = _ffn(h2, w["ffn2_norm"], w["ffn2_wg"], w["ffn2_wu"], w["ffn2_wd"], tm=tm)
    return y, kb32, vb32, new_s, new_buf


def kernel(x_prompt, x_sample, cache_k, cache_v, page_table, state_delta, state_conv, ffn1_norm, ffn1_w_gate, ffn1_w_up, ffn1_w_down, mix_norm, w_in, conv_w, A_log, dt_bias, gdn_out_norm, q_norm, k_norm, lambda_q1, lambda_k1, lambda_q2, lambda_k2, diff_subln, w_out, ffn2_norm, ffn2_w_gate, ffn2_w_up, ffn2_w_down):
    depth = ffn1_norm.shape[0]
    bp, sp, _ = x_prompt.shape
    assert bp == 1
    bs, ss, _ = x_sample.shape
    n_pages = page_table.shape[1]
    n_pool = cache_k.shape[1]
    hw = 2 * DH
    width = H_B * hw

    yp = x_prompt.reshape(bp * sp, D_MODEL)
    ys = x_sample.reshape(bs * ss, D_MODEL)
    def page_rows(cache):
        x = cache.reshape(depth * n_pool, PAGE_SIZE, H_B, 2, DH)
        return jnp.transpose(x, (0, 1, 3, 2, 4)).reshape(depth * n_pool * PAGE_SIZE * 2 * H_B, DH)

    cache_k2 = page_rows(cache_k)
    cache_v2 = page_rows(cache_v)
    outs = [[] for _ in range(8)]
    for l in range(depth):
        lam_init = 0.8 - 0.6 * math.exp(-0.3 * l)
        wl = w_in[l]
        small = jnp.zeros((D_MODEL, LANES), F32).at[:, :2 * H_A].set(wl[:, 4 * D_A:4 * D_A + 2 * H_A])
        off = 4 * D_A + 2 * H_A
        w = {
            "ffn1_norm": ffn1_norm[l], "mix_norm": mix_norm[l], "ffn2_norm": ffn2_norm[l],
            "ffn1_wg": ffn1_w_gate[l].astype(BF16), "ffn1_wu": ffn1_w_up[l].astype(BF16),
            "ffn1_wd": ffn1_w_down[l].astype(BF16),
            "ffn2_wg": ffn2_w_gate[l].astype(BF16), "ffn2_wu": ffn2_w_up[l].astype(BF16),
            "ffn2_wd": ffn2_w_down[l].astype(BF16),
            "w_qkvz": wl[:, :4 * D_A].astype(BF16), "w_gate": small.astype(BF16),
            "w_qb": wl[:, off:off + D_B].astype(BF16), "w_kb": wl[:, off + D_B:off + 2 * D_B].astype(BF16),
            "w_vb": wl[:, off + 2 * D_B:off + 3 * D_B].astype(BF16),
            "w_out_a": w_out[l, :D_A].astype(BF16), "w_out_b": w_out[l, D_A:].astype(BF16),
            "conv_w": conv_w[l], "A_log": A_log[l], "dt_bias": dt_bias[l], "gdn_out_norm": gdn_out_norm[l],
            "q_norm": q_norm[l], "k_norm": k_norm[l],
        }
        lam_rows = jnp.zeros((8, DH), F32)
        lam_rows = lam_rows.at[0].set(lambda_q1[l]).at[1].set(lambda_k1[l]).at[2].set(lambda_q2[l]).at[3].set(lambda_k2[l])
        subln = diff_subln[l]

        tq = 512

        def attend_prompt(q, k, vt):
            return _attn_prompt(q, k, vt, lam_rows, subln, lam_init, tq=tq)

        def attend_sample(q, k, v):
            q4 = q.reshape(bs, ss, 2 * H_B, DH)
            eye = jnp.eye(2 * H_B, dtype=BF16)
            qbd = (q4[:, :, :, None, :] * eye[None, None, :, :, None])
            qbd = jnp.transpose(qbd, (0, 2, 1, 3, 4)).reshape(bs, 2 * H_B * ss, width)
            k_new = jnp.pad(k.reshape(bs, ss, width), ((0, 0), (0, 16 - ss), (0, 0)))
            v_new = jnp.pad(v.reshape(bs, ss, width), ((0, 0), (0, 16 - ss), (0, 0)))
            page_ids = (page_table + l * n_pool).reshape(-1).astype(jnp.int32)
            o = _attn_sample(qbd, k_new, v_new, cache_k2, cache_v2, page_ids, lam_rows, subln, lam_init,
                             n_pages=n_pages, group=16)
            return o.reshape(bs * ss, width)

        buf0 = jnp.zeros((bp, CONV_W - 1, 3 * D_A), F32)
        s0 = jnp.zeros((bp, H_A, DH, DH), F32)
        yp, kp, vp, s_p, b_p = _layer(yp, buf0, s0, attend_prompt, w, batch=bp, seq=sp, tm=512,
                                      gdn_tb=128, v_block=tq)
        ys, k_s, v_s, s_s, b_s = _layer(ys, state_conv[l], state_delta[l], attend_sample, w, batch=bs, seq=ss,
                                        tm=bs * ss, gdn_tb=ss)
        def cache_view(rows, b, s):
            return jnp.transpose(rows.reshape(b, s, 2, H_B, DH), (0, 1, 3, 2, 4)).reshape(b, s, H_B, hw)

        for lst, val in zip(outs, (cache_view(kp, bp, sp), cache_view(vp, bp, sp),
                                   cache_view(k_s, bs, ss), cache_view(v_s, bs, ss),
                                   s_p, b_p, s_s, b_s)):
            lst.append(val)
    return (yp.reshape(bp, sp, D_MODEL), ys.reshape(bs, ss, D_MODEL), *[jnp.stack(o) for o in outs])
```

```python
import functools
import math

import jax
import jax.numpy as jnp
from jax import lax
from jax.experimental import pallas as pl
from jax.experimental.pallas import tpu as pltpu

F32 = jnp.float32
BF16 = jnp.bfloat16

D_MODEL = 2048
D_A = 1024
D_B = 1024
DH = 128
H_A = D_A // DH
H_B = D_B // (2 * DH)
CONV_W = 4
D_FF = 5632
EPS = 1e-6
PAGE_SIZE = 128
LANES = 128
BF16_ROWS = 16
LOG2E = math.log2(math.e)
GATE_LANE = 8
TRI_BASE = 16
CONV_HIST = BF16_ROWS

MIB = 1024 * 1024


def _params(semantics, vmem_mib):
    return pltpu.CompilerParams(dimension_semantics=semantics, vmem_limit_bytes=vmem_mib * MIB)


def _rms(x, g):
    return x * lax.rsqrt(jnp.mean(x * x, axis=-1, keepdims=True) + EPS) * g


def _silu(x):
    return x * jax.nn.sigmoid(x)


def _softplus(x):
    return jnp.maximum(x, 0.0) + jnp.log1p(jnp.exp(-jnp.abs(x)))


def _ffn_body(x_ref, g_ref, wg_ref, wu_ref, wd_ref, *rest, nf, with_norm):
    if with_norm:
        ng_ref, o_ref, on_ref, xn_ref = rest
    else:
        o_ref, xn_ref = rest
    f = pl.program_id(1)

    @pl.when(f == 0)
    def _():
        xn_ref[...] = _rms(x_ref[...], g_ref[...]).astype(BF16)
        o_ref[...] = jnp.zeros_like(o_ref)

    xn = xn_ref[...]
    a = jnp.dot(xn, wg_ref[...], preferred_element_type=F32)
    b = jnp.dot(xn, wu_ref[...], preferred_element_type=F32)
    hid = (_silu(a) * b).astype(BF16)
    o_ref[...] += jnp.dot(hid, wd_ref[...], preferred_element_type=F32)

    @pl.when(f == nf - 1)
    def _():
        y = x_ref[...] + 0.5 * o_ref[...]
        o_ref[...] = y
        if with_norm:
            on_ref[...] = _rms(y, ng_ref[...]).astype(BF16)


def _ffn(x, g, wg, wu, wd, next_g=None, *, tm, tf=512, vmem_mib=48):
    t, d = x.shape
    nf = D_FF // tf
    with_norm = next_g is not None
    in_specs = [
        pl.BlockSpec((tm, d), lambda i, f: (i, 0)),
        pl.BlockSpec((1, d), lambda i, f: (0, 0)),
        pl.BlockSpec((d, tf), lambda i, f: (0, f)),
        pl.BlockSpec((d, tf), lambda i, f: (0, f)),
        pl.BlockSpec((tf, d), lambda i, f: (f, 0)),
    ]
    args = [x, g.reshape(1, d), wg, wu, wd]
    out_shape = [jax.ShapeDtypeStruct((t, d), F32)]
    out_specs = [pl.BlockSpec((tm, d), lambda i, f: (i, 0))]
    if with_norm:
        in_specs.append(pl.BlockSpec((1, d), lambda i, f: (0, 0)))
        args.append(next_g.reshape(1, d))
        out_shape.append(jax.ShapeDtypeStruct((t, d), BF16))
        out_specs.append(pl.BlockSpec((tm, d), lambda i, f: (i, 0)))
    res = pl.pallas_call(
        functools.partial(_ffn_body, nf=nf, with_norm=with_norm),
        grid=(t // tm, nf),
        in_specs=in_specs,
        out_specs=out_specs,
        out_shape=out_shape,
        scratch_shapes=[pltpu.VMEM((tm, d), BF16)],
        compiler_params=_params(("parallel", "arbitrary"), vmem_mib),
        name="ffn",
    )(*args)
    return res if with_norm else res[0]


def _store_cache_rows(o_ref, y, c):
    head, half = divmod(c, 2)
    o_ref[pl.ds(half * H_B + head, y.shape[0], stride=2 * H_B), :] = y


def _proj_body(x_ref, w_ref, *rest, mode):
    acc = jnp.dot(x_ref[...], w_ref[...], preferred_element_type=F32)
    groups = [(c, slice(c * DH, (c + 1) * DH)) for c in range(acc.shape[1] // DH)]
    if mode == "plain":
        (o_ref,) = rest
        o_ref[...] = acc
    elif mode == "qnorm":
        g_ref, o_ref = rest
        for c, cols in groups:
            o_ref[:, cols] = (_rms(acc[:, cols], g_ref[...]) * (LOG2E * DH ** -0.5)).astype(BF16)
    elif mode == "knorm":
        g_ref, o32_ref, o16_ref = rest
        for c, cols in groups:
            y = _rms(acc[:, cols], g_ref[...])
            _store_cache_rows(o32_ref, y, c)
            o16_ref[:, cols] = y.astype(BF16)
    else:
        o32_ref, o16_ref = rest
        for c, cols in groups:
            _store_cache_rows(o32_ref, acc[:, cols], c)
        if mode == "dual":
            o16_ref[...] = acc.astype(BF16)
        else:
            tk = o16_ref.shape[2]
            for r in range(o16_ref.shape[0]):
                o16_ref[r] = acc[r * tk:(r + 1) * tk, :].T.astype(BF16)


def _proj(xn, w, mode, gain=None, *, tm, tn, tk=None):
    t, d = xn.shape
    n = w.shape[1]
    in_specs = [pl.BlockSpec((tm, d), lambda i, j: (i, 0)), pl.BlockSpec((d, tn), lambda i, j: (0, j))]
    args = [xn, w]
    if mode in ("qnorm", "knorm"):
        in_specs.append(pl.BlockSpec((1, DH), lambda i, j: (0, 0)))
        args.append(gain.reshape(1, DH))
    o_spec = pl.BlockSpec((tm, tn), lambda i, j: (i, j))
    if mode == "plain":
        out_shape, out_specs = [jax.ShapeDtypeStruct((t, n), F32)], [o_spec]
    elif mode == "qnorm":
        out_shape, out_specs = [jax.ShapeDtypeStruct((t, n), BF16)], [o_spec]
    else:
        assert tn == n == H_B * 2 * DH
        rows = n // DH
        out_shape = [jax.ShapeDtypeStruct((t * rows, DH), F32)]
        out_specs = [pl.BlockSpec((tm * rows, DH), lambda i, j: (i, 0))]
        if mode == "dual_t":
            out_shape.append(jax.ShapeDtypeStruct((t // tk, n, tk), BF16))
            out_specs.append(pl.BlockSpec((tm // tk, tn, tk), lambda i, j: (i, j, 0)))
        else:
            out_shape.append(jax.ShapeDtypeStruct((t, n), BF16))
            out_specs.append(o_spec)
    return pl.pallas_call(
        functools.partial(_proj_body, mode=mode),
        grid=(t // tm, n // tn),
        in_specs=in_specs,
        out_specs=out_specs,
        out_shape=out_shape,
        compiler_params=_params(("parallel", "arbitrary"), 40),
        name="proj_" + mode,
    )(*args)


def _split(x):
    hi = x.astype(BF16)
    lo = (x - hi.astype(F32)).astype(BF16)
    return hi, lo


def _dot3(a, b, dims=(((1,), (0,)), ((), ()))):
    ah, al = _split(a)
    bh, bl = _split(b)
    dg = functools.partial(lax.dot_general, dimension_numbers=dims, preferred_element_type=F32)
    return dg(ah, bh) + (dg(ah, bl) + dg(al, bh))


_NT = (((1,), (1,)), ((), ()))
_TN = (((0,), (0,)), ((), ()))


_B_NN = (((2,), (1,)), ((0,), (0,)))
_B_NT = (((2,), (2,)), ((0,), (0,)))
_B_TN = (((1,), (1,)), ((0,), (0,)))


def _bdot(a, b, dims=_B_NN):
    return lax.dot_general(a.astype(BF16), b.astype(BF16), dims, preferred_element_type=F32)


def _tri_inv(a):
    c = a.shape[-1]
    base = min(TRI_BASE, c)
    row = lax.broadcasted_iota(jnp.int32, (c, c), 0)
    col = lax.broadcasted_iota(jnp.int32, (c, c), 1)
    eye = (row == col).astype(F32)

    def neumann(m, order):
        x = eye - m
        p = m
        k = 1
        while 2 * k < order:
            p = _bdot(p, p)
            x = x + _bdot(x, p)
            k *= 2
        return x

    if c == base:
        return neumann(a, base)
    shift = base.bit_length() - 1
    same = (row >> shift) == (col >> shift)
    dinv = neumann(jnp.where(same, a, 0.0), base)
    e = _bdot(dinv, jnp.where(same, 0.0, a))
    return _bdot(neumann(e, c // base), dinv)


def _gdn_body(qkv_ref, z_ref, gt_ref, buf_ref, cw_ref, ad_ref, gn_ref, s0_ref,
              o_ref, s_ref, xp_ref, *, tb):
    t = pl.program_id(1)

    @pl.when(t == 0)
    def _():
        s_ref[...] = s0_ref[...]
        xp_ref[0:CONV_HIST, :] = buf_ref[0]

    x = qkv_ref[...]
    xp_ref[CONV_HIST:CONV_HIST + tb, :] = x
    if tb >= LANES:
        r = lax.broadcasted_iota(jnp.int32, ((CONV_W - 1) * tb, tb + CONV_HIST), 0)
        c = lax.broadcasted_iota(jnp.int32, ((CONV_W - 1) * tb, tb + CONV_HIST), 1)
        shift = 1 + lax.shift_right_logical(r, tb.bit_length() - 1)
        sel = (c == CONV_HIST + (r & (tb - 1)) - shift).astype(BF16)
        prev = jnp.dot(sel, xp_ref[...].astype(BF16), preferred_element_type=F32)
        y = cw_ref[0:1, :] * prev[2 * tb:3 * tb]
        y = y + cw_ref[1:2, :] * prev[tb:2 * tb]
        y = y + cw_ref[2:3, :] * prev[0:tb]
    else:
        y = cw_ref[0:1, :] * xp_ref[CONV_HIST - 3:CONV_HIST - 3 + tb, :]
        y = y + cw_ref[1:2, :] * xp_ref[CONV_HIST - 2:CONV_HIST - 2 + tb, :]
        y = y + cw_ref[2:3, :] * xp_ref[CONV_HIST - 1:CONV_HIST - 1 + tb, :]
    y = _silu(y + cw_ref[3:4, :] * x)
    xp_ref[0:CONV_HIST, :] = xp_ref[tb:tb + CONV_HIST, :]

    def heads(base):
        return jnp.stack([y[:, base + h * DH:base + (h + 1) * DH] for h in range(H_A)])

    def l2(x):
        return x * lax.rsqrt(jnp.sum(x * x, axis=-1, keepdims=True) + EPS)

    q = l2(heads(0)) * (DH ** -0.5)
    k = l2(heads(D_A))
    v = heads(2 * D_A)

    gt = gt_ref[...]
    beta = jax.nn.sigmoid(gt)
    g = -jnp.exp(ad_ref[0:1, :]) * _softplus(gt + ad_ref[1:2, :])
    row = lax.broadcasted_iota(jnp.int32, (tb, tb), 0)
    col = lax.broadcasted_iota(jnp.int32, (tb, tb), 1)
    tri_incl = col <= row
    tri_strict = col < row
    gc = _dot3(tri_incl.astype(F32), g)
    pad = max(LANES - tb, 0)
    gc_rows = jnp.concatenate([gc, jnp.zeros((pad, LANES), F32)], axis=0) if pad else gc
    gct = gc_rows.T

    bcol = jnp.stack([beta[:, h:h + 1] for h in range(H_A)])
    gcol = jnp.stack([gc[:, GATE_LANE + h:GATE_LANE + h + 1] for h in range(H_A)])
    grow = jnp.stack([gct[GATE_LANE + h:GATE_LANE + h + 1, 0:tb] for h in range(H_A)])
    decay = jnp.where(tri_incl, jnp.exp(jnp.where(tri_incl, gcol - grow, 0.0)), 0.0)
    egc = jnp.exp(gcol)
    kb = k * bcol

    amat = jnp.where(tri_strict, _bdot(kb, k, _B_NT) * decay, 0.0)
    tinv = _tri_inv(amat)
    uw = _bdot(tinv, jnp.concatenate([v * bcol, kb * egc], axis=-1))
    u, w = uw[:, :, :DH], uw[:, :, DH:]
    qk = jnp.where(tri_incl, _bdot(q, k, _B_NT) * decay, 0.0)

    s = s_ref[0]
    wq_s = _bdot(jnp.concatenate([w, q * egc], axis=1), s)
    v_new = u - wq_s[:, :tb]
    o = wq_s[:, tb:] + _bdot(qk, v_new)
    g_last = gcol[:, tb - 1:tb, :]
    k_dec = k * jnp.exp(g_last - gcol)
    s_ref[0] = s * jnp.exp(g_last) + _bdot(k_dec, v_new, _B_TN)

    on = _rms(o, gn_ref[...])
    for h in range(H_A):
        gate = _silu(z_ref[:, h * DH:(h + 1) * DH])
        o_ref[:, h * DH:(h + 1) * DH] = (on[h] * gate).astype(BF16)


def _gdn(ua, gates, hist, conv_w, a_log, dt_bias, out_norm, s0, *, batch, seq, tb):
    nt = seq // tb
    ad = jnp.zeros((8, LANES), F32)
    ad = ad.at[0, GATE_LANE:GATE_LANE + H_A].set(a_log).at[1, GATE_LANE:GATE_LANE + H_A].set(dt_bias)
    cw = jnp.zeros((8, 3 * D_A), F32).at[:CONV_W].set(conv_w)
    return pl.pallas_call(
        functools.partial(_gdn_body, tb=tb),
        grid=(batch, nt),
        in_specs=[
            pl.BlockSpec((tb, 3 * D_A), lambda b, t: (b * nt + t, 0)),
            pl.BlockSpec((tb, D_A), lambda b, t: (b * nt + t, 3)),
            pl.BlockSpec((tb, LANES), lambda b, t: (b * nt + t, 0)),
            pl.BlockSpec((1, CONV_HIST, 3 * D_A), lambda b, t: (b, 0, 0)),
            pl.BlockSpec((8, 3 * D_A), lambda b, t: (0, 0)),
            pl.BlockSpec((8, LANES), lambda b, t: (0, 0)),
            pl.BlockSpec((1, DH), lambda b, t: (0, 0)),
            pl.BlockSpec((1, H_A, DH, DH), lambda b, t: (b, 0, 0, 0)),
        ],
        out_specs=[
            pl.BlockSpec((tb, D_A), lambda b, t: (b * nt + t, 0)),
            pl.BlockSpec((1, H_A, DH, DH), lambda b, t: (b, 0, 0, 0)),
        ],
        out_shape=[
            jax.ShapeDtypeStruct((batch * seq, D_A), BF16),
            jax.ShapeDtypeStruct((batch, H_A, DH, DH), F32),
        ],
        scratch_shapes=[pltpu.VMEM((tb + CONV_HIST, 3 * D_A), F32)],
        compiler_params=_params(("parallel", "arbitrary"), 40),
        name="gdn",
    )(ua, ua, gates, hist, cw, ad, out_norm.reshape(1, DH), s0)


def _lambda(lam_ref, lam_init):
    l1 = jnp.sum(lam_ref[0:1, :] * lam_ref[1:2, :], axis=-1, keepdims=True)
    l2 = jnp.sum(lam_ref[2:3, :] * lam_ref[3:4, :], axis=-1, keepdims=True)
    return jnp.exp(l1) - jnp.exp(l2) + lam_init


def _attn_prompt_body(lam_ref, sg_ref, q_ref, k_ref, vt_ref, o_ref, acc_ref, s_ref, *, tq, lam_init):
    qi = pl.program_id(1)
    hw = 2 * DH
    acc_ref[...] = jnp.zeros(acc_ref.shape, F32)
    ones = jnp.ones((acc_ref.shape[1] - hw, tq), BF16)

    def scores(j, sub):
        cols = slice(sub * DH, (sub + 1) * DH)
        ks = k_ref[pl.ds(pl.multiple_of(j * tq, tq), tq), cols]
        return lax.dot_general(ks, q_ref[:, cols], _NT, preferred_element_type=F32)

    def accumulate(j, sub, st, m_prev, masked=False):
        if masked:
            key = lax.broadcasted_iota(jnp.int32, (tq, tq), 0)
            qry = lax.broadcasted_iota(jnp.int32, (tq, tq), 1)
            st = jnp.where(key <= qry, st, -jnp.inf)
        vt = jnp.concatenate([vt_ref[j], ones], axis=0)
        m_new = jnp.maximum(m_prev, jnp.max(st, axis=0, keepdims=True))
        corr = jnp.exp2(m_prev - m_new)
        p = jnp.exp2(st - m_new).astype(BF16)
        acc_ref[sub] = acc_ref[sub] * corr + jnp.dot(vt, p, preferred_element_type=F32)
        return m_new

    s_ref[0] = scores(0, 0)
    s_ref[1] = scores(0, 1)

    def pair(p, maxes):
        a, b = 2 * p, 2 * p + 1
        st_b0 = scores(b, 0)
        m0 = accumulate(a, 0, s_ref[0], maxes[0])
        st_b1 = scores(b, 1)
        m1 = accumulate(a, 1, s_ref[1], maxes[1])
        s_ref[0] = scores(b + 1, 0)
        m0 = accumulate(b, 0, st_b0, m0)
        s_ref[1] = scores(b + 1, 1)
        return m0, accumulate(b, 1, st_b1, m1)

    init = jnp.full((1, tq), -jnp.inf, F32)
    n_pairs = lax.shift_right_logical(qi, 1)
    maxes = lax.fori_loop(0, n_pairs, pair, (init, init))

    @pl.when((qi & 1) == 1)
    def _():
        st_d0 = scores(qi, 0)
        m0 = accumulate(qi - 1, 0, s_ref[0], maxes[0])
        st_d1 = scores(qi, 1)
        m1 = accumulate(qi - 1, 1, s_ref[1], maxes[1])
        accumulate(qi, 0, st_d0, m0, masked=True)
        accumulate(qi, 1, st_d1, m1, masked=True)

    @pl.when((qi & 1) == 0)
    def _():
        accumulate(qi, 0, s_ref[0], maxes[0], masked=True)
        accumulate(qi, 1, s_ref[1], maxes[1], masked=True)

    lam = _lambda(lam_ref, lam_init)
    a0, a1 = acc_ref[0], acc_ref[1]
    ot = a0[:hw] / a0[hw:hw + 1] - lam * (a1[:hw] / a1[hw:hw + 1])
    o_ref[...] = (_rms(ot.T, sg_ref[...]) * (1.0 - lam_init)).astype(BF16)


def _attn_prompt(q, k, vt, lam_rows, subln, lam_init, *, tq):
    s_len = q.shape[0]
    hw = 2 * DH
    return pl.pallas_call(
        functools.partial(_attn_prompt_body, tq=tq, lam_init=lam_init),
        grid=(H_B, s_len // tq),
        in_specs=[
            pl.BlockSpec((8, DH), lambda h, i: (0, 0)),
            pl.BlockSpec((1, hw), lambda h, i: (0, 0)),
            pl.BlockSpec((tq, hw), lambda h, i: (i, h)),
            pl.BlockSpec((s_len, hw), lambda h, i: (0, h)),
            pl.BlockSpec((s_len // tq, hw, tq), lambda h, i: (0, h, 0)),
        ],
        out_specs=pl.BlockSpec((tq, hw), lambda h, i: (i, h)),
        out_shape=jax.ShapeDtypeStruct((s_len, H_B * hw), BF16),
        scratch_shapes=[pltpu.VMEM((2, hw + BF16_ROWS, tq), F32), pltpu.VMEM((2, tq, tq), F32)],
        compiler_params=_params(("parallel", "arbitrary"), 52),
        name="attn_prompt",
    )(lam_rows, subln.reshape(1, hw), q, k, vt)


def _attn_sample_body(pt_ref, lam_ref, sg_ref, q_ref, kn_ref, vn_ref, *rest, group, n_steps, n_new, lam_init):
    kc_refs = rest[:group]
    vc_refs = rest[group:2 * group]
    o_ref, m_ref, l_ref, acc_ref = rest[2 * group:]
    step = pl.program_id(1)
    rows = q_ref.shape[1]
    q = q_ref[0]

    def update(s, vals):
        m_prev = m_ref[:, 0:1]
        m_new = jnp.maximum(m_prev, jnp.max(s, axis=-1, keepdims=True))
        corr = jnp.exp2(m_prev - m_new)
        p = jnp.exp2(s - m_new)
        l_new = corr * l_ref[:, 0:1] + jnp.sum(p, axis=-1, keepdims=True)
        pb = p.astype(BF16)
        pv = None
        width = s.shape[1] // len(vals)
        for i, val in enumerate(vals):
            part = jnp.dot(pb[:, i * width:(i + 1) * width], val, preferred_element_type=F32)
            pv = part if pv is None else pv + part
        acc_ref[...] = acc_ref[...] * corr + pv
        m_ref[...] = jnp.broadcast_to(m_new, m_ref.shape)
        l_ref[...] = jnp.broadcast_to(l_new, l_ref.shape)

    @pl.when(step == 0)
    def _():
        m_ref[...] = jnp.full(m_ref.shape, -jnp.inf, F32)
        l_ref[...] = jnp.zeros(l_ref.shape, F32)
        acc_ref[...] = jnp.zeros(acc_ref.shape, F32)
        kn = kn_ref[0]
        s = lax.dot_general(q, kn, _NT, preferred_element_type=F32)
        tok = lax.broadcasted_iota(jnp.int32, s.shape, 0) & (n_new - 1)
        key = lax.broadcasted_iota(jnp.int32, s.shape, 1)
        s = jnp.where(key <= tok, s, -jnp.inf)
        update(s, [vn_ref[0]])

    def page(ref):
        parts = [ref[pl.ds(half * H_B + h, PAGE_SIZE, stride=2 * H_B), :] for h in range(H_B) for half in range(2)]
        return jnp.concatenate(parts, axis=1).astype(BF16)

    s_parts = [lax.dot_general(q, page(kc), _NT, preferred_element_type=F32) for kc in kc_refs]
    update(jnp.concatenate(s_parts, axis=-1), [page(vc) for vc in vc_refs])

    @pl.when(step == n_steps - 1)
    def _():
        lam = _lambda(lam_ref, lam_init)
        hw = 2 * DH
        for h in range(H_B):
            r0 = h * 2 * n_new
            a0 = acc_ref[r0:r0 + n_new, h * hw:(h + 1) * hw] / l_ref[r0:r0 + n_new, 0:1]
            a1 = acc_ref[r0 + n_new:r0 + 2 * n_new, h * hw:(h + 1) * hw] / l_ref[r0 + n_new:r0 + 2 * n_new, 0:1]
            o = a0 - lam * a1
            o_ref[0, :, h * hw:(h + 1) * hw] = (_rms(o, sg_ref[...]) * (1.0 - lam_init)).astype(BF16)


def _attn_sample(qbd, k_new, v_new, cache_k, cache_v, page_ids, lam_rows, subln, lam_init, *, n_pages, group):
    batch, rows, width = qbd.shape
    n_new = rows // (2 * H_B)
    n_steps = n_pages // group
    hw = 2 * DH
    page_rows = PAGE_SIZE * 2 * H_B

    def cache_spec(g):
        return pl.BlockSpec((page_rows, DH), lambda b, s, pt: (pt[b * n_pages + s * group + g], 0))

    in_specs = [
        pl.BlockSpec((8, DH), lambda b, s, pt: (0, 0)),
        pl.BlockSpec((1, hw), lambda b, s, pt: (0, 0)),
        pl.BlockSpec((1, rows, width), lambda b, s, pt: (b, 0, 0)),
        pl.BlockSpec((1, k_new.shape[1], width), lambda b, s, pt: (b, 0, 0)),
        pl.BlockSpec((1, v_new.shape[1], width), lambda b, s, pt: (b, 0, 0)),
    ] + [cache_spec(g) for g in range(group)] + [cache_spec(g) for g in range(group)]
    grid_spec = pltpu.PrefetchScalarGridSpec(
        num_scalar_prefetch=1,
        grid=(batch, n_steps),
        in_specs=in_specs,
        out_specs=pl.BlockSpec((1, n_new, width), lambda b, s, pt: (b, 0, 0)),
        scratch_shapes=[
            pltpu.VMEM((rows, LANES), F32),
            pltpu.VMEM((rows, LANES), F32),
            pltpu.VMEM((rows, width), F32),
        ],
    )
    return pl.pallas_call(
        functools.partial(_attn_sample_body, group=group, n_steps=n_steps, n_new=n_new, lam_init=lam_init),
        grid_spec=grid_spec,
        out_shape=jax.ShapeDtypeStruct((batch, n_new, width), BF16),
        compiler_params=_params(("parallel", "arbitrary"), 48),
        name="attn_sample",
    )(page_ids, lam_rows, subln.reshape(1, hw), qbd, k_new, v_new,
      *([cache_k] * group), *([cache_v] * group))


def _outproj_body(h_ref, a_ref, b_ref, wa_ref, wb_ref, o_ref):
    mix = jnp.dot(a_ref[...], wa_ref[...], preferred_element_type=F32)
    mix = mix + jnp.dot(b_ref[...], wb_ref[...], preferred_element_type=F32)
    o_ref[...] = h_ref[...] + mix


def _outproj(h, oa, ob, wa, wb, *, tm, tn=512):
    t, d = h.shape
    return pl.pallas_call(
        _outproj_body,
        grid=(t // tm, d // tn),
        in_specs=[
            pl.BlockSpec((tm, tn), lambda i, j: (i, j)),
            pl.BlockSpec((tm, D_A), lambda i, j: (i, 0)),
            pl.BlockSpec((tm, D_B), lambda i, j: (i, 0)),
            pl.BlockSpec((D_A, tn), lambda i, j: (0, j)),
            pl.BlockSpec((D_B, tn), lambda i, j: (0, j)),
        ],
        out_specs=pl.BlockSpec((tm, tn), lambda i, j: (i, j)),
        out_shape=jax.ShapeDtypeStruct((t, d), F32),
        compiler_params=_params(("parallel", "arbitrary"), 40),
        name="outproj",
    )(h, oa, ob, wa, wb)


def _layer(x, conv_buf, s_delta, attend, w, *, batch, seq, tm, gdn_tb, v_block=None):
    h, hn = _ffn(x, w["ffn1_norm"], w["ffn1_wg"], w["ffn1_wu"], w["ffn1_wd"], w["mix_norm"], tm=tm)
    ptm = min(2 * tm, x.shape[0])
    (ua,) = _proj(hn, w["w_qkvz"], "plain", tm=ptm, tn=512)
    (gates,) = _proj(hn, w["w_gate"], "plain", tm=ptm, tn=LANES)
    (qb,) = _proj(hn, w["w_qb"], "qnorm", w["q_norm"], tm=ptm, tn=512)
    kb32, kb16 = _proj(hn, w["w_kb"], "knorm", w["k_norm"], tm=ptm, tn=D_B)
    if v_block is None:
        vb32, vb16 = _proj(hn, w["w_vb"], "dual", tm=ptm, tn=D_B)
    else:
        vb32, vb16 = _proj(hn, w["w_vb"], "dual_t", tm=ptm, tn=D_B, tk=v_block)

    hist = jnp.pad(conv_buf, ((0, 0), (CONV_HIST - (CONV_W - 1), 0), (0, 0)))
    oa, new_s = _gdn(ua, gates, hist, w["conv_w"], w["A_log"], w["dt_bias"], w["gdn_out_norm"], s_delta,
                     batch=batch, seq=seq, tb=gdn_tb)
    assert seq >= CONV_W - 1
    new_buf = ua.reshape(batch, seq, 4 * D_A)[:, seq - (CONV_W - 1):, :3 * D_A]

    ob = attend(qb, kb16, vb16)
    h2 = _outproj(h, oa, ob, w["w_out_a"], w["w_out_b"], tm=ptm)
    y = _ffn(h2, w["ffn2_norm"], w["ffn2_wg"], w["ffn2_wu"], w["ffn2_wd"], tm=ptm, vmem_mib=56)
    return y, kb32, vb32, new_s, new_buf


def kernel(x_prompt, x_sample, cache_k, cache_v, page_table, state_delta, state_conv, ffn1_norm, ffn1_w_gate, ffn1_w_up, ffn1_w_down, mix_norm, w_in, conv_w, A_log, dt_bias, gdn_out_norm, q_norm, k_norm, lambda_q1, lambda_k1, lambda_q2, lambda_k2, diff_subln, w_out, ffn2_norm, ffn2_w_gate, ffn2_w_up, ffn2_w_down):
    depth = ffn1_norm.shape[0]
    bp, sp, _ = x_prompt.shape
    assert bp == 1
    bs, ss, _ = x_sample.shape
    n_pages = page_table.shape[1]
    n_pool = cache_k.shape[1]
    hw = 2 * DH
    width = H_B * hw

    yp = x_prompt.reshape(bp * sp, D_MODEL)
    ys = x_sample.reshape(bs * ss, D_MODEL)
    def page_rows(cache):
        x = cache.reshape(depth * n_pool, PAGE_SIZE, H_B, 2, DH)
        return jnp.transpose(x, (0, 1, 3, 2, 4)).reshape(depth * n_pool * PAGE_SIZE * 2 * H_B, DH)

    cache_k2 = page_rows(cache_k)
    cache_v2 = page_rows(cache_v)
    outs = [[] for _ in range(8)]
    for l in range(depth):
        lam_init = 0.8 - 0.6 * math.exp(-0.3 * l)
        wl = w_in[l]
        small = jnp.zeros((D_MODEL, LANES), F32).at[:, :2 * H_A].set(wl[:, 4 * D_A:4 * D_A + 2 * H_A])
        off = 4 * D_A + 2 * H_A
        w = {
            "ffn1_norm": ffn1_norm[l], "mix_norm": mix_norm[l], "ffn2_norm": ffn2_norm[l],
            "ffn1_wg": ffn1_w_gate[l].astype(BF16), "ffn1_wu": ffn1_w_up[l].astype(BF16),
            "ffn1_wd": ffn1_w_down[l].astype(BF16),
            "ffn2_wg": ffn2_w_gate[l].astype(BF16), "ffn2_wu": ffn2_w_up[l].astype(BF16),
            "ffn2_wd": ffn2_w_down[l].astype(BF16),
            "w_qkvz": wl[:, :4 * D_A].astype(BF16), "w_gate": small.astype(BF16),
            "w_qb": wl[:, off:off + D_B].astype(BF16), "w_kb": wl[:, off + D_B:off + 2 * D_B].astype(BF16),
            "w_vb": wl[:, off + 2 * D_B:off + 3 * D_B].astype(BF16),
            "w_out_a": w_out[l, :D_A].astype(BF16), "w_out_b": w_out[l, D_A:].astype(BF16),
            "conv_w": conv_w[l], "A_log": A_log[l], "dt_bias": dt_bias[l], "gdn_out_norm": gdn_out_norm[l],
            "q_norm": q_norm[l], "k_norm": k_norm[l],
        }
        lam_rows = jnp.zeros((8, DH), F32)
        lam_rows = lam_rows.at[0].set(lambda_q1[l]).at[1].set(lambda_k1[l]).at[2].set(lambda_q2[l]).at[3].set(lambda_k2[l])
        subln = diff_subln[l]

        tq = 512

        def attend_prompt(q, k, vt):
            return _attn_prompt(q, k, vt, lam_rows, subln, lam_init, tq=tq)

        def attend_sample(q, k, v):
            q4 = q.reshape(bs, ss, 2 * H_B, DH)
            eye = jnp.eye(2 * H_B, dtype=BF16)
            qbd = (q4[:, :, :, None, :] * eye[None, None, :, :, None])
            qbd = jnp.transpose(qbd, (0, 2, 1, 3, 4)).reshape(bs, 2 * H_B * ss, width)
            k_new = jnp.pad(k.reshape(bs, ss, width), ((0, 0), (0, 16 - ss), (0, 0)))
            v_new = jnp.pad(v.reshape(bs, ss, width), ((0, 0), (0, 16 - ss), (0, 0)))
            page_ids = (page_table + l * n_pool).reshape(-1).astype(jnp.int32)
            o = _attn_sample(qbd, k_new, v_new, cache_k2, cache_v2, page_ids, lam_rows, subln, lam_init,
                             n_pages=n_pages, group=16)
            return o.reshape(bs * ss, width)

        buf0 = jnp.zeros((bp, CONV_W - 1, 3 * D_A), F32)
        s0 = jnp.zeros((bp, H_A, DH, DH), F32)
        yp, kp, vp, s_p, b_p = _layer(yp, buf0, s0, attend_prompt, w, batch=bp, seq=sp, tm=512,
                                      gdn_tb=128, v_block=tq)
        ys, k_s, v_s, s_s, b_s = _layer(ys, state_conv[l], state_delta[l], attend_sample, w, batch=bs, seq=ss,
                                        tm=bs * ss, gdn_tb=ss)
        def cache_view(rows, b, s):
            return jnp.transpose(rows.reshape(b, s, 2, H_B, DH), (0, 1, 3, 2, 4)).reshape(b, s, H_B, hw)

        for lst, val in zip(outs, (cache_view(kp, bp, sp), cache_view(vp, bp, sp),
                                   cache_view(k_s, bs, ss), cache_view(v_s, bs, ss),
                                   s_p, b_p, s_s, b_s)):
            lst.append(val)
    return (yp.reshape(bp, sp, D_MODEL), ys.reshape(bs, ss, D_MODEL), *[jnp.stack(o) for o in outs])
```
